```python
import math, functools
import jax, jax.numpy as jnp
from jax import lax
import numpy as np

D_MODEL = 1024
BATCH = 8
SEQ = 2048
DEPTH = 2
DEC_BATCH = 1
DEC_SEQ = 16384
PAST_LEN = 128

EPS = 1e-6
D_A = 1024
CONV_WIDTH = 31
CONV_PAD = CONV_WIDTH // 2
D_B = 1024
CHUNK = 128
H_B = 8
DH_B = D_B // H_B
D_C = 1024
POOL_WINDOWS = (2, 4, 8, 16)
N_POOL_GROUPS = len(POOL_WINDOWS)
G_C = D_C // N_POOL_GROUPS
D_BRANCH = 1024
N_BRANCH = 3
N_IN = 2 * D_A + 2 * D_B + D_C + N_BRANCH * D_MODEL
N_EXPERTS = 16
D_EXPERT = 2048
CAPACITY_FACTOR = 2

kernel_name = "hybrid_gated_conv_sgu_pool_ec_moe_encoder"


def _rmsnorm(x, g):
    x32 = x.astype(jnp.float32)
    y = x32 * lax.rsqrt(jnp.mean(x32 * x32, axis=-1, keepdims=True) + EPS)
    return (y * g.astype(jnp.float32)).astype(x.dtype)


def _layernorm(x, g, b):
    x32 = x.astype(jnp.float32)
    mu = jnp.mean(x32, axis=-1, keepdims=True)
    xc = x32 - mu
    var = jnp.mean(xc * xc, axis=-1, keepdims=True)
    y = xc * lax.rsqrt(var + EPS)
    return (y * g.astype(jnp.float32) + b.astype(jnp.float32)).astype(x.dtype)


def _conformer_conv(a_pair, conv_w, conv_b, ln_g, ln_b):
    a, gate = jnp.split(a_pair, 2, axis=-1)
    glu = a * jax.nn.sigmoid(gate)
    y = lax.conv_general_dilated(
        glu, conv_w[:, None, :].astype(glu.dtype),
        window_strides=(1,), padding=[(CONV_PAD, CONV_PAD)],
        dimension_numbers=("NWC", "WIO", "NWC"),
        feature_group_count=D_A) + conv_b
    y = _layernorm(y, ln_g, ln_b)
    return jax.nn.silu(y)


def _spatial_gating(u, v, ln_g, ln_b, w_s, b_s):
    bsz, s, _ = v.shape
    v = _layernorm(v, ln_g, ln_b)
    v = v.reshape(bsz, s // CHUNK, CHUNK, H_B, DH_B)
    vm = jnp.einsum("hpq,bnqhd->bnphd", w_s.astype(v.dtype), v) + b_s.T[:, :, None].astype(v.dtype)
    return u * vm.reshape(bsz, s, D_B)


def _multiscale_pool(xc, pool_w, pool_scale):
    bsz, s, _ = xc.shape
    x32 = xc.astype(jnp.float32)
    cs = jnp.concatenate([jnp.zeros((bsz, 1, D_C), jnp.float32), jnp.cumsum(x32, axis=1)], axis=1)
    t = jnp.arange(s)
    outs = []
    for g, w in enumerate(POOL_WINDOWS):
        lo = jnp.clip(t - w // 2, 0, s)
        hi = jnp.clip(t - w // 2 + w, 0, s)
        seg = cs[:, :, g * G_C:(g + 1) * G_C]
        win_sum = jnp.take(seg, hi, axis=1) - jnp.take(seg, lo, axis=1)
        mean = win_sum / (hi - lo).astype(jnp.float32)[None, :, None]
        diff = mean - x32[:, :, g * G_C:(g + 1) * G_C]
        outs.append(jnp.einsum("bsc,cd->bsd", diff, pool_w[g].astype(jnp.float32)))
    y = jnp.concatenate(outs, axis=-1) * pool_scale.astype(jnp.float32)
    return y.astype(xc.dtype)


def _expert_choice_moe(h, w_router, w_gate, w_up, w_down):
    n_tok = h.shape[0]
    cap = max(1, CAPACITY_FACTOR * n_tok // N_EXPERTS)
    logits = jnp.einsum("td,de->te", h.astype(jnp.float32), w_router.astype(jnp.float32))
    aff = jax.nn.softmax(logits, axis=-1)
    vals, idx = lax.top_k(aff.T, cap)
    xs = jnp.take(h, idx, axis=0)
    hid = jax.nn.silu(jnp.einsum("ecd,edf->ecf", xs, w_gate)) * jnp.einsum("ecd,edf->ecf", xs, w_up)
    out = jnp.einsum("ecf,efd->ecd", hid, w_down) * vals[..., None].astype(h.dtype)
    return jnp.zeros_like(h).at[idx.reshape(-1)].add(out.reshape(-1, h.shape[1]))


def _trunk(x, c, w_ada, b_ada, g_norm1, g_norm2, w_in, b_in, conv_w, conv_b, conv_ln_g, conv_ln_b,
           sgu_ln_g, sgu_ln_b, sgu_w, sgu_b, pool_w, pool_scale, w_branch, w_out,
           w_router, w_e_gate, w_e_up, w_e_down, g_final):
    bsz, s, d = x.shape
    for l in range(DEPTH):
        mod = jnp.einsum("bd,de->be", jax.nn.silu(c), w_ada[l]) + b_ada[l]
        sh1, sc1, gt1, sh2, sc2, gt2 = [m[:, None, :] for m in jnp.split(mod, 6, axis=-1)]

        h = _rmsnorm(x, g_norm1[l]) * (1 + sc1) + sh1
        proj = jnp.einsum("bsd,dn->bsn", h, w_in[l]) + b_in[l]
        o0 = 2 * D_A
        o1 = o0 + D_B
        o2 = o1 + D_B
        o3 = o2 + D_C
        a_pair = proj[..., :o0]
        u_b = proj[..., o0:o1]
        v_b = proj[..., o1:o2]
        x_c = proj[..., o2:o3]
        gates = jax.nn.sigmoid(proj[..., o3:].reshape(bsz, s, N_BRANCH, d))

        y_a = _conformer_conv(a_pair, conv_w[l], conv_b[l], conv_ln_g[l], conv_ln_b[l])
        y_b = _spatial_gating(u_b, v_b, sgu_ln_g[l], sgu_ln_b[l], sgu_w[l], sgu_b[l])
        y_c = _multiscale_pool(x_c, pool_w[l], pool_scale[l])
        branches = jnp.stack([y_a, y_b, y_c], axis=2)
        merged = jnp.einsum("bsik,ikd,bsid->bsd", branches, w_branch[l], gates) if False else \
            jnp.sum(jnp.einsum("bsik,ikd->bsid", branches, w_branch[l]) * gates, axis=2)
        mix = jnp.einsum("bsd,de->bse", merged, w_out[l])
        x = x + gt1 * mix

        h2 = _rmsnorm(x, g_norm2[l]) * (1 + sc2) + sh2
        y2 = _expert_choice_moe(h2.reshape(bsz * s, d), w_router[l], w_e_gate[l], w_e_up[l], w_e_down[l])
        x = x + gt2 * y2.reshape(bsz, s, d)
    return _rmsnorm(x, g_final)


def setup_inputs(seed: int = 0) -> dict:
    key = jax.random.key(seed)
    ks = jax.random.split(key, 32)
    f32 = jnp.float32

    def nrm(k, shape, scale):
        return jax.random.normal(k, shape, f32) * scale

    d = D_MODEL
    return {
        "x_prompt": nrm(ks[0], (BATCH, SEQ, d), 1.0),
        "x_sample": nrm(ks[1], (DEC_BATCH, DEC_SEQ, d), 1.0),
        "c_prompt": nrm(ks[2], (BATCH, d), 1.0),
        "c_sample": nrm(ks[3], (DEC_BATCH, d), 1.0),
        "w_ada": nrm(ks[4], (DEPTH, d, 6 * d), 0.5 * d ** -0.5),
        "b_ada": nrm(ks[5], (DEPTH, 6 * d), 0.02),
        "g_norm1": 1.0 + nrm(ks[6], (DEPTH, d), 0.02),
        "g_norm2": 1.0 + nrm(ks[7], (DEPTH, d), 0.02),
        "w_in": nrm(ks[8], (DEPTH, d, N_IN), d ** -0.5),
        "b_in": nrm(ks[9], (DEPTH, N_IN), 0.02),
        "conv_w": nrm(ks[10], (DEPTH, CONV_WIDTH, D_A), CONV_WIDTH ** -0.5),
        "conv_b": nrm(ks[11], (DEPTH, D_A), 0.02),
        "conv_ln_g": 1.0 + nrm(ks[12], (DEPTH, D_A), 0.02),
        "conv_ln_b": nrm(ks[13], (DEPTH, D_A), 0.02),
        "sgu_ln_g": 1.0 + nrm(ks[14], (DEPTH, D_B), 0.02),
        "sgu_ln_b": nrm(ks[15], (DEPTH, D_B), 0.02),
        "sgu_w": nrm(ks[16], (DEPTH, H_B, CHUNK, CHUNK), 0.5 * CHUNK ** -0.5),
        "sgu_b": 1.0 + nrm(ks[17], (DEPTH, H_B, CHUNK), 0.02),
        "pool_w": nrm(ks[18], (DEPTH, N_POOL_GROUPS, G_C, G_C), G_C ** -0.5),
        "pool_scale": 1.0 + nrm(ks[19], (DEPTH, D_C), 0.02),
        "w_branch": nrm(ks[20], (DEPTH, N_BRANCH, D_BRANCH, d), D_BRANCH ** -0.5),
        "w_out": nrm(ks[21], (DEPTH, d, d), d ** -0.5),
        "w_router": nrm(ks[22], (DEPTH, d, N_EXPERTS), d ** -0.5),
        "w_e_gate": nrm(ks[23], (DEPTH, N_EXPERTS, d, D_EXPERT), d ** -0.5),
        "w_e_up": nrm(ks[24], (DEPTH, N_EXPERTS, d, D_EXPERT), d ** -0.5),
        "w_e_down": nrm(ks[25], (DEPTH, N_EXPERTS, D_EXPERT, d), D_EXPERT ** -0.5),
        "g_final": 1.0 + nrm(ks[26], (d,), 0.02),
    }


def reference(x_prompt, x_sample, c_prompt, c_sample, w_ada, b_ada, g_norm1, g_norm2, w_in, b_in,
              conv_w, conv_b, conv_ln_g, conv_ln_b, sgu_ln_g, sgu_ln_b, sgu_w, sgu_b,
              pool_w, pool_scale, w_branch, w_out, w_router, w_e_gate, w_e_up, w_e_down, g_final):
    y_prompt = _trunk(x_prompt, c_prompt, w_ada, b_ada, g_norm1, g_norm2, w_in, b_in, conv_w, conv_b,
                      conv_ln_g, conv_ln_b, sgu_ln_g, sgu_ln_b, sgu_w, sgu_b, pool_w, pool_scale,
                      w_branch, w_out, w_router, w_e_gate, w_e_up, w_e_down, g_final)
    y_sample = _trunk(x_sample, c_sample, w_ada, b_ada, g_norm1, g_norm2, w_in, b_in, conv_w, conv_b,
                      conv_ln_g, conv_ln_b, sgu_ln_g, sgu_ln_b, sgu_w, sgu_b, pool_w, pool_scale,
                      w_branch, w_out, w_router, w_e_gate, w_e_up, w_e_down, g_final)
    return (y_prompt, y_sample)
```

```python
import functools

import jax
import jax.numpy as jnp
from jax import lax
from jax.experimental import pallas as pl
from jax.experimental.pallas import tpu as pltpu

EPS = 1e-6
CONV_WIDTH = 31
CONV_PAD = CONV_WIDTH // 2
SGU_CHUNK = 128
SGU_HEADS = 8
POOL_WINDOWS = (2, 4, 8, 16)
N_BRANCH = 3
N_EXPERTS = 16
CAPACITY_FACTOR = 2

LANES = 128
SUBLANES = 8
HALO = 16
MIX_TILE = 256
TOK_TILE = 128
SLOT_CHUNK = 256
FFN_ROWS = 1024
FFN_COLS = 512
VMEM_LIMIT = 56 * 1024 * 1024
NOT_SELECTED = -(1 << 20)

_F32 = jnp.float32
_BF16 = jnp.bfloat16


def _dot(a, b):
    return jnp.dot(a, b, preferred_element_type=_F32)


def _sigmoid(x):
    return 1.0 / (1.0 + jnp.exp(-x))


def _rms_mod(x, scale, shift):
    return x * lax.rsqrt(jnp.mean(x * x, axis=-1, keepdims=True) + EPS) * scale + shift


def _layernorm(x, g, b):
    mu = jnp.mean(x, axis=-1, keepdims=True)
    xc = x - mu
    var = jnp.mean(xc * xc, axis=-1, keepdims=True)
    return xc * lax.rsqrt(var + EPS) * g + b


def _adaln_kernel(c_ref, w_ref, b_ref, o_ref):
    c = c_ref[...]
    s = c * _sigmoid(c)
    o_ref[0] = jnp.dot(s, w_ref[0], preferred_element_type=_F32,
                       precision=lax.Precision.HIGHEST) + b_ref[0]


def _adaln(c_all, w_ada, b_ada):
    depth, d, n = w_ada.shape
    rows = c_all.shape[0]
    return pl.pallas_call(
        _adaln_kernel,
        out_shape=jax.ShapeDtypeStruct((depth, rows, n), _F32),
        grid=(depth, n // d),
        in_specs=[
            pl.BlockSpec((rows, d), lambda l, j: (0, 0)),
            pl.BlockSpec((1, d, d), lambda l, j: (l, 0, j)),
            pl.BlockSpec((1, 1, d), lambda l, j: (l, 0, j)),
        ],
        out_specs=pl.BlockSpec((1, rows, d), lambda l, j: (l, 0, j)),
        name="adaln",
    )(c_all, w_ada, b_ada.reshape(depth, 1, n))


def _mixer_kernel(xp_ref, x_ref, xn_ref, mod_ref, g1_ref, g2_ref, win_ref, bin_ref,
                  convw_ref, convb_ref, clng_ref, clnb_ref, slng_ref, slnb_ref, sguw_ref, sgub_ref,
                  poolw_ref, pscale_ref, wbr_ref, wout_ref, wrt_ref,
                  x1_ref, h2_ref, aff_ref,
                  hbuf, gbuf, cbuf, ybuf, *, seq_len):
    tm = x_ref.shape[1]
    d = x_ref.shape[2]
    i = pl.program_id(1)
    n_tiles = pl.num_programs(1)

    sh1 = mod_ref[0, 0:1, :]
    sc1 = mod_ref[0, 1:2, :]
    gt1 = mod_ref[0, 2:3, :]
    sh2 = mod_ref[0, 3:4, :]
    sc2 = mod_ref[0, 4:5, :]
    scale1 = g1_ref[...] * (1.0 + sc1)
    scale2 = g2_ref[...] * (1.0 + sc2)

    xm = x_ref[0]
    hbuf[0:HALO, :] = _rms_mod(xp_ref[0], scale1, sh1).astype(_BF16)
    hbuf[HALO:HALO + tm, :] = _rms_mod(xm, scale1, sh1).astype(_BF16)
    hbuf[HALO + tm:, :] = _rms_mod(xn_ref[0], scale1, sh1).astype(_BF16)

    rows = lax.broadcasted_iota(jnp.int32, (tm + 2 * HALO, 1), 0)
    valid = jnp.logical_and(jnp.logical_or(rows >= HALO, i > 0),
                            jnp.logical_or(rows < tm + HALO, i < n_tiles - 1))

    o_u = 2 * d
    o_v = 3 * d
    o_c = 4 * d
    o_g = 5 * d

    pa = _dot(hbuf[...], win_ref[:, 0:o_u]) + bin_ref[:, 0:o_u]
    glu = pa[:, :d] * _sigmoid(pa[:, d:])
    gbuf[...] = jnp.where(valid, glu, 0.0)
    rc = 2 * SUBLANES
    for c in range(tm // rc):
        r0 = c * rc
        acc = jnp.zeros((rc, d), _F32)
        for k in range(CONV_WIDTH):
            lo = r0 + HALO - CONV_PAD + k
            acc = acc + gbuf[lo:lo + rc, :] * convw_ref[k:k + 1, :]
        y = _layernorm(acc + convb_ref[...], clng_ref[...], clnb_ref[...])
        ybuf[0, r0:r0 + rc, :] = (y * _sigmoid(y)).astype(_BF16)

    hm = hbuf[HALO:HALO + tm, :]
    pu = _dot(hm, win_ref[:, o_u:o_v]) + bin_ref[:, o_u:o_v]
    pv = _dot(hm, win_ref[:, o_v:o_c]) + bin_ref[:, o_v:o_c]
    vn = _layernorm(pv, slng_ref[...], slnb_ref[...]).astype(_BF16)
    dh = d // SGU_HEADS
    for n in range(tm // SGU_CHUNK):
        r0 = n * SGU_CHUNK
        for hd in range(SGU_HEADS):
            c0 = hd * dh
            vm = _dot(sguw_ref[hd], vn[r0:r0 + SGU_CHUNK, c0:c0 + dh]) + sgub_ref[:, c0:c0 + dh]
            ybuf[1, r0:r0 + SGU_CHUNK, c0:c0 + dh] = (pu[r0:r0 + SGU_CHUNK, c0:c0 + dh] * vm).astype(_BF16)

    pc = _dot(hbuf[...], win_ref[:, o_c:o_g]) + bin_ref[:, o_c:o_g]
    cbuf[...] = jnp.where(valid, pc, 0.0)
    pos = i * tm + lax.broadcasted_iota(jnp.int32, (tm, 1), 0)
    gc = d // len(POOL_WINDOWS)
    for g, w in enumerate(POOL_WINDOWS):
        c0 = g * gc
        half = w // 2
        tot = jnp.zeros((tm, gc), _F32)
        for j in range(-half, half):
            tot = tot + cbuf[HALO + j:HALO + j + tm, c0:c0 + gc]
        lo = jnp.maximum(pos - half, 0)
        hi = jnp.minimum(pos - half + w, seq_len)
        mean = tot / (hi - lo).astype(_F32)
        diff = mean - cbuf[HALO:HALO + tm, c0:c0 + gc]
        yc = _dot(diff.astype(_BF16), poolw_ref[g]) * pscale_ref[:, c0:c0 + gc]
        ybuf[2, :, c0:c0 + gc] = yc.astype(_BF16)

    merged = jnp.zeros((tm, d), _F32)
    for b in range(N_BRANCH):
        gate = _sigmoid(_dot(hm, win_ref[:, o_g + b * d:o_g + (b + 1) * d]) + bin_ref[:, o_g + b * d:o_g + (b + 1) * d])
        merged = merged + _dot(ybuf[b], wbr_ref[b]) * gate
    mix = _dot(merged.astype(_BF16), wout_ref[...])
    x1 = xm + gt1 * mix
    x1_ref[0] = x1

    h2 = _rms_mod(x1, scale2, sh2)
    h2_ref[0] = h2.astype(_BF16)
    logits = lax.dot_general(wrt_ref[...], h2, (((1,), (1,)), ((), ())),
                             preferred_element_type=_F32, precision=lax.Precision.HIGHEST)
    m = jnp.max(logits, axis=0, keepdims=True)
    ex = jnp.exp(logits - m)
    aff_ref[...] = ex / jnp.sum(ex, axis=0, keepdims=True)


def _const_spec(shape):
    nd = len(shape)
    return pl.BlockSpec(shape, lambda b, i: (0,) * nd, pipeline_mode=pl.Buffered(1))


def _mixer(x, mod, lw, tm=MIX_TILE):
    bsz, s, d = x.shape
    nt = s // tm
    hb = tm // HALO
    n_in = lw["w_in"].shape[1]
    x_halo = lambda f: pl.BlockSpec((1, HALO, d), f)
    return pl.pallas_call(
        functools.partial(_mixer_kernel, seq_len=s),
        out_shape=(
            jax.ShapeDtypeStruct((bsz, s, d), _F32),
            jax.ShapeDtypeStruct((bsz, s, d), _BF16),
            jax.ShapeDtypeStruct((N_EXPERTS, bsz * s), _F32),
        ),
        grid=(bsz, nt),
        in_specs=[
            x_halo(lambda b, i: (b, jnp.maximum(i * hb - 1, 0), 0)),
            pl.BlockSpec((1, tm, d), lambda b, i: (b, i, 0)),
            x_halo(lambda b, i: (b, jnp.minimum((i + 1) * hb, s // HALO - 1), 0)),
            pl.BlockSpec((1, 6, d), lambda b, i: (b, 0, 0)),
            _const_spec((1, d)), _const_spec((1, d)),
            _const_spec((d, n_in)), _const_spec((1, n_in)),
            _const_spec((CONV_WIDTH, d)), _const_spec((1, d)), _const_spec((1, d)), _const_spec((1, d)),
            _const_spec((1, d)), _const_spec((1, d)),
            _const_spec((SGU_HEADS, SGU_CHUNK, SGU_CHUNK)), _const_spec((SGU_CHUNK, d)),
            _const_spec(lw["pool_w"].shape), _const_spec((1, d)),
            _const_spec((N_BRANCH, d, d)), _const_spec((d, d)), _const_spec((N_EXPERTS, d)),
        ],
        out_specs=(
            pl.BlockSpec((1, tm, d), lambda b, i: (b, i, 0)),
            pl.BlockSpec((1, tm, d), lambda b, i: (b, i, 0)),
            pl.BlockSpec((N_EXPERTS, tm), lambda b, i: (0, b * nt + i)),
        ),
        scratch_shapes=[
            pltpu.VMEM((tm + 2 * HALO, d), _BF16),
            pltpu.VMEM((tm + 2 * HALO, d), _F32),
            pltpu.VMEM((tm + 2 * HALO, d), _F32),
            pltpu.VMEM((N_BRANCH, tm, d), _BF16),
        ],
        compiler_params=pltpu.CompilerParams(
            dimension_semantics=("arbitrary", "arbitrary"), vmem_limit_bytes=VMEM_LIMIT),
        name="mixer",
    )(x, x, x, mod, lw["g1"], lw["g2"], lw["w_in"], lw["b_in"], lw["conv_w"], lw["conv_b"],
      lw["conv_ln_g"], lw["conv_ln_b"], lw["sgu_ln_g"], lw["sgu_ln_b"], lw["sgu_w"], lw["sgu_b"],
      lw["pool_w"], lw["pool_scale"], lw["w_branch"], lw["w_out"], lw["w_router_t"])


def _route_kernel(aff_ref, pos_ref, start_ref, *, cap, n_tok):
    n_e, n_rows, _ = aff_ref.shape
    keys = pltpu.bitcast(aff_ref[...], jnp.int32)

    def count(ind):
        return jnp.sum(jnp.sum(ind, axis=1, keepdims=True), axis=2, keepdims=True)

    def thr_step(it, prefix):
        cand = prefix | jnp.left_shift(jnp.int32(1), 30 - it)
        c = count(jnp.where(keys >= cand, 1.0, 0.0))
        return jnp.where(c >= cap, cand, prefix)

    thr = lax.fori_loop(0, 31, thr_step, jnp.zeros((n_e, 1, 1), jnp.int32))
    gt = keys > thr
    eq = keys == thr
    need = cap - count(jnp.where(gt, 1.0, 0.0))
    tidx = (lax.broadcasted_iota(jnp.int32, (1, n_rows, LANES), 1) * LANES
            + lax.broadcasted_iota(jnp.int32, (1, n_rows, LANES), 2))
    n_bits = max(1, (n_tok - 1).bit_length())

    def tie_step(it, bound):
        cand = bound + jnp.left_shift(jnp.int32(1), n_bits - 1 - it)
        c = count(jnp.where(eq, jnp.where(tidx < cand, 1.0, 0.0), 0.0))
        return jnp.where(c < need, cand, bound)

    bound = lax.fori_loop(0, n_bits, tie_step, jnp.zeros((n_e, 1, 1), jnp.int32))
    sel = jnp.where(gt, 1.0, jnp.where(eq, jnp.where(tidx <= bound, 1.0, 0.0), 0.0))

    r_i = lax.broadcasted_iota(jnp.int32, (LANES, LANES), 0)
    c_i = lax.broadcasted_iota(jnp.int32, (LANES, LANES), 1)
    upper = jnp.where(r_i <= c_i, 1.0, 0.0).astype(_BF16)
    ones = jnp.ones((LANES, LANES), _BF16)
    rr = lax.broadcasted_iota(jnp.int32, (n_rows, n_rows), 0)
    rc = lax.broadcasted_iota(jnp.int32, (n_rows, n_rows), 1)
    lower = jnp.where(rc < rr, 1.0, 0.0).astype(_BF16)
    for e in range(n_e):
        s = sel[e]
        sb = s.astype(_BF16)
        incl = _dot(sb, upper)
        tot = _dot(sb, ones)
        start = _dot(lower, tot.astype(_BF16))
        slot = (start + incl).astype(jnp.int32) - 1
        pos_ref[e] = jnp.where(s > 0.5, slot, NOT_SELECTED)
        start_ref[e] = start.astype(jnp.int32)


def _route(aff_t, cap):
    n_e, n_tok = aff_t.shape
    n_rows = n_tok // LANES
    return pl.pallas_call(
        functools.partial(_route_kernel, cap=cap, n_tok=n_tok),
        out_shape=(
            jax.ShapeDtypeStruct((n_e, n_rows, LANES), jnp.int32),
            jax.ShapeDtypeStruct((n_e, n_rows, LANES), jnp.int32),
        ),
        name="route",
    )(aff_t.reshape(n_e, n_rows, LANES))


def _bits_desc(n):
    return tuple(range(n.bit_length() - 1, -1, -1))


def _dispatch_copies(start_ref, cnt_ref, xbuf, x_hbm, sems, tile, slot, act):
    run = 0
    for e in range(N_EXPERTS):
        cnt = cnt_ref[tile * N_EXPERTS + e]
        st = start_ref[tile * N_EXPERTS + e]
        for b in _bits_desc(TOK_TILE):
            sz = 1 << b
            done = cnt & ~((2 << b) - 1)

            @pl.when((cnt & sz) != 0)
            def _(run=run, done=done, st=st, sz=sz, e=e):
                act(pltpu.make_async_copy(xbuf.at[slot, pl.ds(run + done, sz)],
                                          x_hbm.at[e, pl.ds(st + done, sz)], sems.at[slot]))
        run = run + cnt


def _dispatch_kernel(start_ref, cnt_ref, pos_ref, h2_ref, x_hbm, xbuf, sems):
    i = pl.program_id(0)
    last = pl.num_programs(0) - 1
    slot = i % 2
    tt = h2_ref.shape[0]
    d = h2_ref.shape[1]

    shift = []
    run = 0
    for e in range(N_EXPERTS):
        shift.append(run - start_ref[i * N_EXPERTS + e])
        run = run + cnt_ref[i * N_EXPERTS + e]
    n_rows = run

    for k in range(N_EXPERTS * tt // SLOT_CHUNK):
        @pl.when(k * SLOT_CHUNK < n_rows)
        def _(k=k):
            c_i = k * SLOT_CHUNK + lax.broadcasted_iota(jnp.int32, (SLOT_CHUNK, tt), 0)
            pt = jnp.zeros((SLOT_CHUNK, tt), _F32)
            for e in range(N_EXPERTS):
                pt = jnp.where(c_i == pos_ref[e, 0] + shift[e], 1.0, pt)
            ptb = pt.astype(_BF16)
            for j in range(d // LANES):
                xbuf[slot, pl.ds(k * SLOT_CHUNK, SLOT_CHUNK), j, :] = _dot(ptb, h2_ref[:, j * LANES:(j + 1) * LANES])

    @pl.when(i > 0)
    def _():
        _dispatch_copies(start_ref, cnt_ref, xbuf, x_hbm, sems, i - 1, 1 - slot, lambda cp: cp.wait())

    _dispatch_copies(start_ref, cnt_ref, xbuf, x_hbm, sems, i, slot, lambda cp: cp.start())

    @pl.when(i == last)
    def _():
        _dispatch_copies(start_ref, cnt_ref, xbuf, x_hbm, sems, i, slot, lambda cp: cp.wait())


def _dispatch(h2, pos, start_tab, cnt_tab, cap):
    n_tok, d = h2.shape
    n_e, n_rows, _ = pos.shape
    n_tiles = n_tok // TOK_TILE
    return pl.pallas_call(
        _dispatch_kernel,
        out_shape=jax.ShapeDtypeStruct((n_e, cap, d // LANES, LANES), _F32),
        grid_spec=pltpu.PrefetchScalarGridSpec(
            num_scalar_prefetch=2,
            grid=(n_tiles,),
            in_specs=[
                pl.BlockSpec((n_e, 1, 1, LANES), lambda i, s, c: (0, i, 0, 0)),
                pl.BlockSpec((TOK_TILE, d), lambda i, s, c: (i, 0)),
            ],
            out_specs=pl.BlockSpec(memory_space=pl.ANY),
            scratch_shapes=[
                pltpu.VMEM((2, n_e * TOK_TILE, d // LANES, LANES), _F32),
                pltpu.SemaphoreType.DMA((2,)),
            ],
        ),
        compiler_params=pltpu.CompilerParams(
            dimension_semantics=("arbitrary",), vmem_limit_bytes=VMEM_LIMIT),
        name="dispatch",
    )(start_tab, cnt_tab, pos.reshape(n_e, n_rows, 1, LANES), h2)


def _ffn_kernel(x_ref, wg_ref, wu_ref, wd_ref, y_ref, xb, acc):
    f = pl.program_id(2)
    n_j = x_ref.shape[2]

    @pl.when(f == 0)
    def _():
        for j in range(n_j):
            xb[:, j * LANES:(j + 1) * LANES] = x_ref[0, :, j, :].astype(_BF16)
        acc[...] = jnp.zeros_like(acc)

    x = xb[...]
    g = _dot(x, wg_ref[0, 0].astype(_BF16))
    u = _dot(x, wu_ref[0, 0].astype(_BF16))
    hid = (g * _sigmoid(g) * u).astype(_BF16)
    acc[...] += _dot(hid, wd_ref[0, 0].astype(_BF16))

    @pl.when(f == pl.num_programs(2) - 1)
    def _():
        for j in range(n_j):
            y_ref[0, :, j, :] = acc[:, j * LANES:(j + 1) * LANES]


def _ffn(x_exp, w_gate, w_up, w_down, layer):
    n_e, cap, n_j, _ = x_exp.shape
    d = n_j * LANES
    d_exp = w_gate.shape[-1]
    mt = min(FFN_ROWS, cap)
    fc = min(FFN_COLS, d_exp)
    return pl.pallas_call(
        _ffn_kernel,
        out_shape=jax.ShapeDtypeStruct(x_exp.shape, _F32),
        grid=(n_e, cap // mt, d_exp // fc),
        in_specs=[
            pl.BlockSpec((1, mt, n_j, LANES), lambda e, m, f: (e, m, 0, 0)),
            pl.BlockSpec((1, 1, d, fc), lambda e, m, f: (layer, e, 0, f)),
            pl.BlockSpec((1, 1, d, fc), lambda e, m, f: (layer, e, 0, f)),
            pl.BlockSpec((1, 1, fc, d), lambda e, m, f: (layer, e, f, 0)),
        ],
        out_specs=pl.BlockSpec((1, mt, n_j, LANES), lambda e, m, f: (e, m, 0, 0)),
        scratch_shapes=[pltpu.VMEM((mt, d), _BF16), pltpu.VMEM((mt, d), _F32)],
        compiler_params=pltpu.CompilerParams(
            dimension_semantics=("arbitrary", "arbitrary", "arbitrary"), vmem_limit_bytes=VMEM_LIMIT),
        name="ffn",
    )(x_exp, w_gate, w_up, w_down)


_GROUP_BITS = _bits_desc(TOK_TILE // SUBLANES + 1)
_COMBINE_ROWS = -(-(N_EXPERTS * (TOK_TILE + 2 * SUBLANES)) // SLOT_CHUNK) * SLOT_CHUNK


def _combine_windows(start_ref, cnt_ref, tile):
    out = []
    run = 0
    for e in range(N_EXPERTS):
        cnt = cnt_ref[tile * N_EXPERTS + e]
        st = start_ref[tile * N_EXPERTS + e]
        a0 = (st >> 3) << 3
        a1 = ((st + cnt + SUBLANES - 1) >> 3) << 3
        n8 = jnp.where(cnt > 0, (a1 - a0) >> 3, 0)
        out.append((a0, n8, run))
        run = run + n8 * SUBLANES
    return out, run


def _combine_copies(start_ref, cnt_ref, y_hbm, ybuf, sems, tile, slot, act):
    wins, _ = _combine_windows(start_ref, cnt_ref, tile)
    for e, (a0, n8, boff) in enumerate(wins):
        for b in _GROUP_BITS:
            sz = SUBLANES << b
            done = (n8 & ~((2 << b) - 1)) * SUBLANES

            @pl.when((n8 & (1 << b)) != 0)
            def _(a0=a0, boff=boff, done=done, sz=sz, e=e):
                src = y_hbm.at[e, pl.ds(pl.multiple_of(a0 + done, SUBLANES), sz)]
                dst = ybuf.at[slot, pl.ds(pl.multiple_of(boff + done, SUBLANES), sz)]
                act(pltpu.make_async_copy(src, dst, sems.at[slot]))


def _combine_kernel(start_ref, cnt_ref, pos_ref, aff_ref, x1_ref, gt_ref, gf_ref, y_hbm, o_ref,
                    ybuf, acc, sems, *, final_norm):
    i = pl.program_id(0)
    n_tiles = pl.num_programs(0)
    slot = i % 2
    tt, d = x1_ref.shape

    @pl.when(i == 0)
    def _():
        ybuf[...] = jnp.zeros_like(ybuf)
        _combine_copies(start_ref, cnt_ref, y_hbm, ybuf, sems, i, slot, lambda cp: cp.start())

    @pl.when(i + 1 < n_tiles)
    def _():
        _combine_copies(start_ref, cnt_ref, y_hbm, ybuf, sems, i + 1, 1 - slot, lambda cp: cp.start())

    _combine_copies(start_ref, cnt_ref, y_hbm, ybuf, sems, i, slot, lambda cp: cp.wait())

    wins, n_rows = _combine_windows(start_ref, cnt_ref, i)
    acc[...] = jnp.zeros_like(acc)
    for k in range(_COMBINE_ROWS // SLOT_CHUNK):
        @pl.when(k * SLOT_CHUNK < n_rows)
        def _(k=k):
            l_i = k * SLOT_CHUNK + lax.broadcasted_iota(jnp.int32, (tt, SLOT_CHUNK), 1)
            p = jnp.zeros((tt, SLOT_CHUNK), _F32)
            for e, (a0, _, boff) in enumerate(wins):
                col = pos_ref[:, e:e + 1] + (boff - a0)
                p = jnp.where(l_i == col, aff_ref[:, e:e + 1], p)
            ph = p.astype(_BF16)
            plo = (p - ph.astype(_F32)).astype(_BF16)
            for j in range(d // LANES):
                y = ybuf[slot, pl.ds(k * SLOT_CHUNK, SLOT_CHUNK), j, :]
                yh = y.astype(_BF16)
                ylo = (y - yh.astype(_F32)).astype(_BF16)
                acc[:, j * LANES:(j + 1) * LANES] += _dot(ph, yh) + (_dot(ph, ylo) + _dot(plo, yh))

    x2 = x1_ref[...] + gt_ref[0] * acc[...]
    if final_norm:
        x2 = x2 * lax.rsqrt(jnp.mean(x2 * x2, axis=-1, keepdims=True) + EPS) * gf_ref[...]
    o_ref[...] = x2


def _combine(y_exp, pos_t, aff_tok, x1, gt2, g_final, start_tab, cnt_tab, seq_len, final_norm):
    n_tok, d = x1.shape
    n_e = y_exp.shape[0]
    n_tiles = n_tok // TOK_TILE
    tiles_per_seq = seq_len // TOK_TILE
    return pl.pallas_call(
        functools.partial(_combine_kernel, final_norm=final_norm),
        out_shape=jax.ShapeDtypeStruct((n_tok, d), _F32),
        grid_spec=pltpu.PrefetchScalarGridSpec(
            num_scalar_prefetch=2,
            grid=(n_tiles,),
            in_specs=[
                pl.BlockSpec((TOK_TILE, n_e), lambda i, s, c: (i, 0)),
                pl.BlockSpec((TOK_TILE, n_e), lambda i, s, c: (i, 0)),
                pl.BlockSpec((TOK_TILE, d), lambda i, s, c: (i, 0)),
                pl.BlockSpec((1, 1, d), lambda i, s, c: (i // tiles_per_seq, 0, 0)),
                pl.BlockSpec((1, d), lambda i, s, c: (0, 0)),
                pl.BlockSpec(memory_space=pl.ANY),
            ],
            out_specs=pl.BlockSpec((TOK_TILE, d), lambda i, s, c: (i, 0)),
            scratch_shapes=[
                pltpu.VMEM((2, _COMBINE_ROWS, d // LANES, LANES), _F32),
                pltpu.VMEM((TOK_TILE, d), _F32),
                pltpu.SemaphoreType.DMA((2,)),
            ],
        ),
        compiler_params=pltpu.CompilerParams(
            dimension_semantics=("arbitrary",), vmem_limit_bytes=VMEM_LIMIT),
        name="combine",
    )(start_tab, cnt_tab, pos_t, aff_tok, x1, gt2, g_final, y_exp)


def _moe(x1, h2, aff_t, gt2, g_final, w_gate, w_up, w_down, layer, final_norm):
    bsz, s, d = x1.shape
    n_tok = bsz * s
    n_e = aff_t.shape[0]
    cap = max(1, CAPACITY_FACTOR * n_tok // n_e)
    pos, start_rep = _route(aff_t, cap)
    row_start = start_rep[:, :, 0]
    row_end = jnp.concatenate([row_start[:, 1:], jnp.full((n_e, 1), cap, jnp.int32)], axis=1)
    start_tab = row_start.T.reshape(-1)
    cnt_tab = (row_end - row_start).T.reshape(-1)
    x_exp = _dispatch(h2.reshape(n_tok, d), pos, start_tab, cnt_tab, cap)
    y_exp = _ffn(x_exp, w_gate, w_up, w_down, layer)
    x2 = _combine(y_exp, pos.reshape(n_e, n_tok).T, aff_t.T, x1.reshape(n_tok, d), gt2, g_final,
                  start_tab, cnt_tab, s, final_norm)
    return x2.reshape(bsz, s, d)


def kernel(x_prompt, x_sample, c_prompt, c_sample, w_ada, b_ada, g_norm1, g_norm2, w_in, b_in, conv_w, conv_b, conv_ln_g, conv_ln_b, sgu_ln_g, sgu_ln_b, sgu_w, sgu_b, pool_w, pool_scale, w_branch, w_out, w_router, w_e_gate, w_e_up, w_e_down, g_final):
    depth = w_ada.shape[0]
    d = x_prompt.shape[-1]
    groups = ((x_prompt, c_prompt), (x_sample, c_sample))

    n_c = sum(c.shape[0] for _, c in groups)
    pad = -n_c % SUBLANES
    c_all = jnp.concatenate([c for _, c in groups] + [jnp.zeros((pad, d), _F32)], axis=0)
    mod_all = _adaln(c_all, w_ada, b_ada)

    row = lambda v: v.reshape(1, -1)
    layers = []
    for l in range(depth):
        layers.append(dict(
            g1=row(g_norm1[l]), g2=row(g_norm2[l]),
            w_in=w_in[l].astype(_BF16), b_in=row(b_in[l]),
            conv_w=conv_w[l], conv_b=row(conv_b[l]),
            conv_ln_g=row(conv_ln_g[l]), conv_ln_b=row(conv_ln_b[l]),
            sgu_ln_g=row(sgu_ln_g[l]), sgu_ln_b=row(sgu_ln_b[l]),
            sgu_w=sgu_w[l].astype(_BF16),
            sgu_b=jnp.repeat(sgu_b[l].T, d // SGU_HEADS, axis=1),
            pool_w=pool_w[l].astype(_BF16), pool_scale=row(pool_scale[l]),
            w_branch=w_branch[l].astype(_BF16), w_out=w_out[l].astype(_BF16),
            w_router_t=w_router[l].T,
        ))
    gf = row(g_final)

    outs = []
    c0 = 0
    for x, c in groups:
        bsz = x.shape[0]
        for l in range(depth):
            mod = mod_all[l, c0:c0 + bsz].reshape(bsz, 6, d)
            x1, h2, aff_t = _mixer(x, mod, layers[l])
            x = _moe(x1, h2, aff_t, mod[:, 5:6, :], gf, w_e_gate, w_e_up, w_e_down, l,
                     final_norm=(l == depth - 1))
        outs.append(x)
        c0 += bsz
    return tuple(outs)
```

```python
import functools

import jax
import jax.numpy as jnp
from jax import lax
from jax.experimental import pallas as pl
from jax.experimental.pallas import tpu as pltpu

EPS = 1e-6
CONV_WIDTH = 31
CONV_PAD = CONV_WIDTH // 2
SGU_CHUNK = 128
SGU_HEADS = 8
POOL_WINDOWS = (2, 4, 8, 16)
N_BRANCH = 3
N_EXPERTS = 16
CAPACITY_FACTOR = 2

LANES = 128
SUBLANES = 8
HALO = 16
MIX_TILE = 256
TOK_TILE = 128
DISPATCH_CHUNK = 384
COMBINE_CHUNK = 512
FFN_ROWS = 1024
FFN_COLS = 512
VMEM_LIMIT = 56 * 1024 * 1024
NOT_SELECTED = -(1 << 20)

_F32 = jnp.float32
_BF16 = jnp.bfloat16


def _dot(a, b):
    return jnp.dot(a, b, preferred_element_type=_F32)


def _sigmoid(x):
    return 1.0 / (1.0 + jnp.exp(-x))


def _rms_mod(x, scale, shift):
    return x * lax.rsqrt(jnp.mean(x * x, axis=-1, keepdims=True) + EPS) * scale + shift


def _layernorm(x, g, b):
    mu = jnp.mean(x, axis=-1, keepdims=True)
    xc = x - mu
    var = jnp.mean(xc * xc, axis=-1, keepdims=True)
    return xc * lax.rsqrt(var + EPS) * g + b


def _adaln_kernel(c_ref, w_ref, b_ref, o_ref):
    c = c_ref[...]
    s = c * _sigmoid(c)
    o_ref[0] = jnp.dot(s, w_ref[0], preferred_element_type=_F32,
                       precision=lax.Precision.HIGHEST) + b_ref[0]


def _adaln(c_all, w_ada, b_ada):
    depth, d, n = w_ada.shape
    rows = c_all.shape[0]
    return pl.pallas_call(
        _adaln_kernel,
        out_shape=jax.ShapeDtypeStruct((depth, rows, n), _F32),
        grid=(depth, n // d),
        in_specs=[
            pl.BlockSpec((rows, d), lambda l, j: (0, 0)),
            pl.BlockSpec((1, d, d), lambda l, j: (l, 0, j)),
            pl.BlockSpec((1, 1, d), lambda l, j: (l, 0, j)),
        ],
        out_specs=pl.BlockSpec((1, rows, d), lambda l, j: (l, 0, j)),
        name="adaln",
    )(c_all, w_ada, b_ada.reshape(depth, 1, n))


def _mixer_kernel(xp_ref, x_ref, xn_ref, mod_ref, g1_ref, g2_ref, win_ref, bin_ref,
                  convw_ref, convb_ref, clng_ref, clnb_ref, slng_ref, slnb_ref, sguw_ref, sgub_ref,
                  poolw_ref, pscale_ref, wbr_ref, wout_ref, wrt_ref,
                  x1_ref, h2_ref, aff_ref,
                  hbuf, gbuf, cbuf, ybuf, *, seq_len):
    tm = x_ref.shape[1]
    d = x_ref.shape[2]
    i = pl.program_id(1)
    n_tiles = pl.num_programs(1)

    sh1 = mod_ref[0, 0:1, :]
    sc1 = mod_ref[0, 1:2, :]
    gt1 = mod_ref[0, 2:3, :]
    sh2 = mod_ref[0, 3:4, :]
    sc2 = mod_ref[0, 4:5, :]
    scale1 = g1_ref[...] * (1.0 + sc1)
    scale2 = g2_ref[...] * (1.0 + sc2)

    xm = x_ref[0]
    hbuf[0:HALO, :] = _rms_mod(xp_ref[0], scale1, sh1).astype(_BF16)
    hbuf[HALO:HALO + tm, :] = _rms_mod(xm, scale1, sh1).astype(_BF16)
    hbuf[HALO + tm:, :] = _rms_mod(xn_ref[0], scale1, sh1).astype(_BF16)

    rows = lax.broadcasted_iota(jnp.int32, (tm + 2 * HALO, 1), 0)
    valid = jnp.logical_and(jnp.logical_or(rows >= HALO, i > 0),
                            jnp.logical_or(rows < tm + HALO, i < n_tiles - 1))

    o_u = 2 * d
    o_v = 3 * d
    o_c = 4 * d
    o_g = 5 * d

    pa = _dot(hbuf[...], win_ref[:, 0:o_u]) + bin_ref[:, 0:o_u]
    glu = pa[:, :d] * _sigmoid(pa[:, d:])
    gbuf[...] = jnp.where(valid, glu, 0.0)
    rc = 2 * SUBLANES
    for c in range(tm // rc):
        r0 = c * rc
        acc = jnp.zeros((rc, d), _F32)
        for k in range(CONV_WIDTH):
            lo = r0 + HALO - CONV_PAD + k
            acc = acc + gbuf[lo:lo + rc, :] * convw_ref[k:k + 1, :]
        y = _layernorm(acc + convb_ref[...], clng_ref[...], clnb_ref[...])
        ybuf[0, r0:r0 + rc, :] = (y * _sigmoid(y)).astype(_BF16)

    hm = hbuf[HALO:HALO + tm, :]
    pu = _dot(hm, win_ref[:, o_u:o_v]) + bin_ref[:, o_u:o_v]
    pv = _dot(hm, win_ref[:, o_v:o_c]) + bin_ref[:, o_v:o_c]
    vn = _layernorm(pv, slng_ref[...], slnb_ref[...]).astype(_BF16)
    dh = d // SGU_HEADS
    for n in range(tm // SGU_CHUNK):
        r0 = n * SGU_CHUNK
        for hd in range(SGU_HEADS):
            c0 = hd * dh
            vm = _dot(sguw_ref[hd], vn[r0:r0 + SGU_CHUNK, c0:c0 + dh]) + sgub_ref[:, c0:c0 + dh]
            ybuf[1, r0:r0 + SGU_CHUNK, c0:c0 + dh] = (pu[r0:r0 + SGU_CHUNK, c0:c0 + dh] * vm).astype(_BF16)

    pc = _dot(hbuf[...], win_ref[:, o_c:o_g]) + bin_ref[:, o_c:o_g]
    cbuf[...] = jnp.where(valid, pc, 0.0)
    pos = i * tm + lax.broadcasted_iota(jnp.int32, (tm, 1), 0)
    gc = d // len(POOL_WINDOWS)
    for g, w in enumerate(POOL_WINDOWS):
        c0 = g * gc
        half = w // 2
        tot = jnp.zeros((tm, gc), _F32)
        for j in range(-half, half):
            tot = tot + cbuf[HALO + j:HALO + j + tm, c0:c0 + gc]
        lo = jnp.maximum(pos - half, 0)
        hi = jnp.minimum(pos - half + w, seq_len)
        mean = tot / (hi - lo).astype(_F32)
        diff = mean - cbuf[HALO:HALO + tm, c0:c0 + gc]
        yc = _dot(diff.astype(_BF16), poolw_ref[g]) * pscale_ref[:, c0:c0 + gc]
        ybuf[2, :, c0:c0 + gc] = yc.astype(_BF16)

    merged = jnp.zeros((tm, d), _F32)
    for b in range(N_BRANCH):
        gate = _sigmoid(_dot(hm, win_ref[:, o_g + b * d:o_g + (b + 1) * d]) + bin_ref[:, o_g + b * d:o_g + (b + 1) * d])
        merged = merged + _dot(ybuf[b], wbr_ref[b]) * gate
    mix = _dot(merged.astype(_BF16), wout_ref[...])
    x1 = xm + gt1 * mix
    x1_ref[0] = x1

    h2 = _rms_mod(x1, scale2, sh2)
    h2_ref[0] = h2.astype(_BF16)
    logits = lax.dot_general(wrt_ref[...], h2, (((1,), (1,)), ((), ())),
                             preferred_element_type=_F32, precision=lax.Precision.HIGHEST)
    m = jnp.max(logits, axis=0, keepdims=True)
    ex = jnp.exp(logits - m)
    aff_ref[...] = ex / jnp.sum(ex, axis=0, keepdims=True)


def _const_spec(shape):
    nd = len(shape)
    return pl.BlockSpec(shape, lambda b, i: (0,) * nd, pipeline_mode=pl.Buffered(1))


def _mixer(x, mod, lw, tm=MIX_TILE):
    bsz, s, d = x.shape
    nt = s // tm
    hb = tm // HALO
    n_in = lw["w_in"].shape[1]
    x_halo = lambda f: pl.BlockSpec((1, HALO, d), f)
    return pl.pallas_call(
        functools.partial(_mixer_kernel, seq_len=s),
        out_shape=(
            jax.ShapeDtypeStruct((bsz, s, d), _F32),
            jax.ShapeDtypeStruct((bsz, s, d), _BF16),
            jax.ShapeDtypeStruct((N_EXPERTS, bsz * s), _F32),
        ),
        grid=(bsz, nt),
        in_specs=[
            x_halo(lambda b, i: (b, jnp.maximum(i * hb - 1, 0), 0)),
            pl.BlockSpec((1, tm, d), lambda b, i: (b, i, 0)),
            x_halo(lambda b, i: (b, jnp.minimum((i + 1) * hb, s // HALO - 1), 0)),
            pl.BlockSpec((1, 6, d), lambda b, i: (b, 0, 0)),
            _const_spec((1, d)), _const_spec((1, d)),
            _const_spec((d, n_in)), _const_spec((1, n_in)),
            _const_spec((CONV_WIDTH, d)), _const_spec((1, d)), _const_spec((1, d)), _const_spec((1, d)),
            _const_spec((1, d)), _const_spec((1, d)),
            _const_spec((SGU_HEADS, SGU_CHUNK, SGU_CHUNK)), _const_spec((SGU_CHUNK, d)),
            _const_spec(lw["pool_w"].shape), _const_spec((1, d)),
            _const_spec((N_BRANCH, d, d)), _const_spec((d, d)), _const_spec((N_EXPERTS, d)),
        ],
        out_specs=(
            pl.BlockSpec((1, tm, d), lambda b, i: (b, i, 0)),
            pl.BlockSpec((1, tm, d), lambda b, i: (b, i, 0)),
            pl.BlockSpec((N_EXPERTS, tm), lambda b, i: (0, b * nt + i)),
        ),
        scratch_shapes=[
            pltpu.VMEM((tm + 2 * HALO, d), _BF16),
            pltpu.VMEM((tm + 2 * HALO, d), _F32),
            pltpu.VMEM((tm + 2 * HALO, d), _F32),
            pltpu.VMEM((N_BRANCH, tm, d), _BF16),
        ],
        compiler_params=pltpu.CompilerParams(
            dimension_semantics=("arbitrary", "arbitrary"), vmem_limit_bytes=VMEM_LIMIT),
        name="mixer",
    )(x, x, x, mod, lw["g1"], lw["g2"], lw["w_in"], lw["b_in"], lw["conv_w"], lw["conv_b"],
      lw["conv_ln_g"], lw["conv_ln_b"], lw["sgu_ln_g"], lw["sgu_ln_b"], lw["sgu_w"], lw["sgu_b"],
      lw["pool_w"], lw["pool_scale"], lw["w_branch"], lw["w_out"], lw["w_router_t"])


def _route_kernel(aff_ref, pos_ref, start_ref, *, cap, n_tok):
    n_e, n_rows, _ = aff_ref.shape
    keys = pltpu.bitcast(aff_ref[...], jnp.int32)

    def count(ind):
        return jnp.sum(jnp.sum(ind, axis=1, keepdims=True), axis=2, keepdims=True)

    def thr_step(it, prefix):
        cand = prefix | jnp.left_shift(jnp.int32(1), 30 - it)
        c = count(jnp.where(keys >= cand, 1.0, 0.0))
        return jnp.where(c >= cap, cand, prefix)

    thr = lax.fori_loop(0, 31, thr_step, jnp.zeros((n_e, 1, 1), jnp.int32))
    gt = keys > thr
    eq = keys == thr
    need = cap - count(jnp.where(gt, 1.0, 0.0))
    tidx = (lax.broadcasted_iota(jnp.int32, (1, n_rows, LANES), 1) * LANES
            + lax.broadcasted_iota(jnp.int32, (1, n_rows, LANES), 2))
    n_bits = max(1, (n_tok - 1).bit_length())

    def tie_step(it, bound):
        cand = bound + jnp.left_shift(jnp.int32(1), n_bits - 1 - it)
        c = count(jnp.where(eq, jnp.where(tidx < cand, 1.0, 0.0), 0.0))
        return jnp.where(c < need, cand, bound)

    bound = lax.fori_loop(0, n_bits, tie_step, jnp.zeros((n_e, 1, 1), jnp.int32))
    sel = jnp.where(gt, 1.0, jnp.where(eq, jnp.where(tidx <= bound, 1.0, 0.0), 0.0))

    r_i = lax.broadcasted_iota(jnp.int32, (LANES, LANES), 0)
    c_i = lax.broadcasted_iota(jnp.int32, (LANES, LANES), 1)
    upper = jnp.where(r_i <= c_i, 1.0, 0.0).astype(_BF16)
    ones = jnp.ones((LANES, LANES), _BF16)
    rr = lax.broadcasted_iota(jnp.int32, (n_rows, n_rows), 0)
    rc = lax.broadcasted_iota(jnp.int32, (n_rows, n_rows), 1)
    lower = jnp.where(rc < rr, 1.0, 0.0).astype(_BF16)
    for e in range(n_e):
        s = sel[e]
        sb = s.astype(_BF16)
        incl = _dot(sb, upper)
        tot = _dot(sb, ones)
        start = _dot(lower, tot.astype(_BF16))
        slot = (start + incl).astype(jnp.int32) - 1
        pos_ref[e] = jnp.where(s > 0.5, slot, NOT_SELECTED)
        start_ref[e] = start.astype(jnp.int32)


def _route(aff_t, cap):
    n_e, n_tok = aff_t.shape
    n_rows = n_tok // LANES
    return pl.pallas_call(
        functools.partial(_route_kernel, cap=cap, n_tok=n_tok),
        out_shape=(
            jax.ShapeDtypeStruct((n_e, n_rows, LANES), jnp.int32),
            jax.ShapeDtypeStruct((n_e, n_rows, LANES), jnp.int32),
        ),
        name="route",
    )(aff_t.reshape(n_e, n_rows, LANES))


def _round_up(n, m):
    return -(-n // m) * m


_DISPATCH_ROWS = _round_up(N_EXPERTS * TOK_TILE, DISPATCH_CHUNK)


def _dispatch_copies(start_ref, cnt_ref, xbuf, x_hbm, sems, tile, slot, act):
    run = 0
    for e in range(N_EXPERTS):
        cnt = cnt_ref[tile * N_EXPERTS + e]
        st = start_ref[tile * N_EXPERTS + e]

        @pl.when(cnt > 0)
        def _(run=run, cnt=cnt, st=st, e=e):
            src = xbuf.at[slot, pl.ds(pl.multiple_of(run * SUBLANES, SUBLANES), cnt * SUBLANES)]
            dst = x_hbm.at[e, pl.ds(pl.multiple_of(st * SUBLANES, SUBLANES), cnt * SUBLANES)]
            act(pltpu.make_async_copy(src, dst, sems.at[slot]))
        run = run + cnt


def _dispatch_kernel(start_ref, cnt_ref, pos_ref, h2_ref, x_hbm, xbuf, sems):
    i = pl.program_id(0)
    last = pl.num_programs(0) - 1
    slot = i % 2
    tt, d = h2_ref.shape

    shift = []
    run = 0
    for e in range(N_EXPERTS):
        shift.append(run - start_ref[i * N_EXPERTS + e])
        run = run + cnt_ref[i * N_EXPERTS + e]
    n_rows = run

    for k in range(_DISPATCH_ROWS // DISPATCH_CHUNK):
        @pl.when(k * DISPATCH_CHUNK < n_rows)
        def _(k=k):
            c_i = k * DISPATCH_CHUNK + lax.broadcasted_iota(jnp.int32, (DISPATCH_CHUNK, tt), 0)
            pt = jnp.zeros((DISPATCH_CHUNK, tt), _F32)
            for e in range(N_EXPERTS):
                pt = jnp.where(c_i == pos_ref[e, 0] + shift[e], 1.0, pt)
            ptb = pt.astype(_BF16)
            for j2 in range(d // (2 * LANES)):
                r = _dot(ptb, h2_ref[:, j2 * 2 * LANES:(j2 + 1) * 2 * LANES])
                for h in range(2):
                    rows = pl.ds(k * DISPATCH_CHUNK * SUBLANES + 2 * j2 + h, DISPATCH_CHUNK, stride=SUBLANES)
                    xbuf[slot, rows, :] = r[:, h * LANES:(h + 1) * LANES]

    @pl.when(i > 0)
    def _():
        _dispatch_copies(start_ref, cnt_ref, xbuf, x_hbm, sems, i - 1, 1 - slot, lambda cp: cp.wait())

    _dispatch_copies(start_ref, cnt_ref, xbuf, x_hbm, sems, i, slot, lambda cp: cp.start())

    @pl.when(i == last)
    def _():
        _dispatch_copies(start_ref, cnt_ref, xbuf, x_hbm, sems, i, slot, lambda cp: cp.wait())


def _dispatch(h2, pos, start_tab, cnt_tab, cap):
    n_tok, d = h2.shape
    n_e, n_rows, _ = pos.shape
    assert d == SUBLANES * LANES, "one activation row must fill one 8 x 128 tile"
    n_tiles = n_tok // TOK_TILE
    return pl.pallas_call(
        _dispatch_kernel,
        out_shape=jax.ShapeDtypeStruct((n_e, cap * SUBLANES, LANES), _F32),
        grid_spec=pltpu.PrefetchScalarGridSpec(
            num_scalar_prefetch=2,
            grid=(n_tiles,),
            in_specs=[
                pl.BlockSpec((n_e, 1, 1, LANES), lambda i, s, c: (0, i, 0, 0)),
                pl.BlockSpec((TOK_TILE, d), lambda i, s, c: (i, 0)),
            ],
            out_specs=pl.BlockSpec(memory_space=pl.ANY),
            scratch_shapes=[
                pltpu.VMEM((2, _DISPATCH_ROWS * SUBLANES, LANES), _F32),
                pltpu.SemaphoreType.DMA((2,)),
            ],
        ),
        compiler_params=pltpu.CompilerParams(
            dimension_semantics=("arbitrary",), vmem_limit_bytes=VMEM_LIMIT),
        name="dispatch",
    )(start_tab, cnt_tab, pos.reshape(n_e, n_rows, 1, LANES), h2)


def _ffn_kernel(x_ref, wg_ref, wu_ref, wd_ref, y_ref, xb):
    f = pl.program_id(2)
    n_j = SUBLANES
    mt = xb.shape[0]

    @pl.when(f == 0)
    def _():
        for j in range(n_j):
            xb[:, j * LANES:(j + 1) * LANES] = x_ref[0, pl.ds(j, mt, stride=SUBLANES), :].astype(_BF16)

    x = xb[...]
    g = _dot(x, wg_ref[0, 0].astype(_BF16))
    u = _dot(x, wu_ref[0, 0].astype(_BF16))
    hid = (g * _sigmoid(g) * u).astype(_BF16)
    res = _dot(hid, wd_ref[0, 0].astype(_BF16))

    @pl.when(f == 0)
    def _():
        for j in range(n_j):
            y_ref[0, j] = res[:, j * LANES:(j + 1) * LANES]

    @pl.when(f > 0)
    def _():
        for j in range(n_j):
            y_ref[0, j] += res[:, j * LANES:(j + 1) * LANES]


def _ffn(x_exp, w_gate, w_up, w_down, layer):
    n_e = x_exp.shape[0]
    n_j = SUBLANES
    cap = x_exp.shape[1] // n_j
    d = n_j * LANES
    d_exp = w_gate.shape[-1]
    mt = min(FFN_ROWS, cap)
    fc = min(FFN_COLS, d_exp)
    return pl.pallas_call(
        _ffn_kernel,
        out_shape=jax.ShapeDtypeStruct((n_e, n_j, cap, LANES), _F32),
        grid=(n_e, cap // mt, d_exp // fc),
        in_specs=[
            pl.BlockSpec((1, mt * n_j, LANES), lambda e, m, f: (e, m, 0)),
            pl.BlockSpec((1, 1, d, fc), lambda e, m, f: (layer, e, 0, f)),
            pl.BlockSpec((1, 1, d, fc), lambda e, m, f: (layer, e, 0, f)),
            pl.BlockSpec((1, 1, fc, d), lambda e, m, f: (layer, e, f, 0)),
        ],
        out_specs=pl.BlockSpec((1, n_j, mt, LANES), lambda e, m, f: (e, 0, m, 0)),
        scratch_shapes=[pltpu.VMEM((mt, d), _BF16)],
        compiler_params=pltpu.CompilerParams(
            dimension_semantics=("arbitrary", "arbitrary", "arbitrary"), vmem_limit_bytes=VMEM_LIMIT),
        name="ffn",
    )(x_exp, w_gate, w_up, w_down)


_COMBINE_ROWS = _round_up(N_EXPERTS * (TOK_TILE + 2 * SUBLANES), COMBINE_CHUNK)
_COL_BITS = 6
_COL_SPLIT = 1 << _COL_BITS
_NO_COLUMN = _COL_SPLIT * _COL_SPLIT - 1
assert _COMBINE_ROWS <= _NO_COLUMN


def _combine_windows(start_ref, cnt_ref, tile):
    out = []
    run = 0
    for e in range(N_EXPERTS):
        cnt = cnt_ref[tile * N_EXPERTS + e]
        st = start_ref[tile * N_EXPERTS + e]
        a0 = (st >> 3) << 3
        a1 = ((st + cnt + SUBLANES - 1) >> 3) << 3
        rows = jnp.where(cnt > 0, a1 - a0, 0)
        out.append((a0, rows, run))
        run = run + rows
    return out, run


def _combine_copies(start_ref, cnt_ref, y_hbm, ybuf, sems, tile, slot, act):
    wins, _ = _combine_windows(start_ref, cnt_ref, tile)
    for e, (a0, rows, boff) in enumerate(wins):
        @pl.when(rows > 0)
        def _(a0=a0, rows=rows, boff=boff, e=e):
            n = pl.multiple_of(rows, SUBLANES)
            src = y_hbm.at[e, :, pl.ds(pl.multiple_of(a0, SUBLANES), n), :]
            dst = ybuf.at[slot, :, pl.ds(pl.multiple_of(boff, SUBLANES), n), :]
            act(pltpu.make_async_copy(src, dst, sems.at[slot]))


def _combine_kernel(start_ref, cnt_ref, pos_ref, aff_ref, x1_ref, gt_ref, gf_ref, y_hbm, o_ref,
                    ybuf, acc, sems, *, final_norm):
    i = pl.program_id(0)
    n_tiles = pl.num_programs(0)
    slot = i % 2
    tt, d = x1_ref.shape

    @pl.when(i == 0)
    def _():
        ybuf[...] = jnp.zeros_like(ybuf)
        _combine_copies(start_ref, cnt_ref, y_hbm, ybuf, sems, i, slot, lambda cp: cp.start())

    @pl.when(i + 1 < n_tiles)
    def _():
        _combine_copies(start_ref, cnt_ref, y_hbm, ybuf, sems, i + 1, 1 - slot, lambda cp: cp.start())

    _combine_copies(start_ref, cnt_ref, y_hbm, ybuf, sems, i, slot, lambda cp: cp.wait())

    wins, n_rows = _combine_windows(start_ref, cnt_ref, i)
    n_e = pos_ref.shape[1]
    ch = COMBINE_CHUNK

    lane_e = lax.broadcasted_iota(jnp.int32, (1, n_e), 1)
    shift = jnp.zeros((1, n_e), jnp.int32)
    for e, (a0, _, boff) in enumerate(wins):
        shift = jnp.where(lane_e == e, boff - a0, shift)
    pos = pos_ref[...]
    col = jnp.where(pos >= 0, pos + shift, _NO_COLUMN)
    col_hi = (col >> _COL_BITS).astype(_F32).astype(_BF16)
    col_lo = (col & (_COL_SPLIT - 1)).astype(_F32).astype(_BF16)
    aff = aff_ref[...]
    a_hi = aff.astype(_BF16)
    a_lo = (aff - a_hi.astype(_F32)).astype(_BF16)

    acc[...] = jnp.zeros_like(acc)
    for k in range(_COMBINE_ROWS // ch):
        @pl.when(k * ch < n_rows)
        def _(k=k):
            c_i = k * ch + lax.broadcasted_iota(jnp.int32, (1, ch), 1)
            owner = jnp.zeros((1, ch), jnp.int32)
            for _, rows, boff in wins:
                owner = owner + jnp.where(c_i >= boff + rows, 1, 0)
            spread = jnp.where(lax.broadcasted_iota(jnp.int32, (n_e, ch), 0) == owner, 1.0, 0.0).astype(_BF16)
            col_c = _COL_SPLIT * _dot(col_hi, spread) + _dot(col_lo, spread)
            hit = col_c == c_i.astype(_F32)
            ph = jnp.where(hit, _dot(a_hi, spread), 0.0).astype(_BF16)
            plo = jnp.where(hit, _dot(a_lo, spread), 0.0).astype(_BF16)
            for j2 in range(d // (2 * LANES)):
                y = jnp.concatenate([ybuf[slot, 2 * j2, pl.ds(k * ch, ch), :],
                                     ybuf[slot, 2 * j2 + 1, pl.ds(k * ch, ch), :]], axis=1)
                yh = y.astype(_BF16)
                ylo = (y - yh.astype(_F32)).astype(_BF16)
                acc[:, j2 * 2 * LANES:(j2 + 1) * 2 * LANES] += _dot(ph, yh) + (_dot(ph, ylo) + _dot(plo, yh))

    x2 = x1_ref[...] + gt_ref[0] * acc[...]
    if final_norm:
        x2 = x2 * lax.rsqrt(jnp.mean(x2 * x2, axis=-1, keepdims=True) + EPS) * gf_ref[...]
    o_ref[...] = x2


def _combine(y_exp, pos_t, aff_tok, x1, gt2, g_final, start_tab, cnt_tab, seq_len, final_norm):
    n_tok, d = x1.shape
    n_e = y_exp.shape[0]
    n_tiles = n_tok // TOK_TILE
    tiles_per_seq = seq_len // TOK_TILE
    return pl.pallas_call(
        functools.partial(_combine_kernel, final_norm=final_norm),
        out_shape=jax.ShapeDtypeStruct((n_tok, d), _F32),
        grid_spec=pltpu.PrefetchScalarGridSpec(
            num_scalar_prefetch=2,
            grid=(n_tiles,),
            in_specs=[
                pl.BlockSpec((TOK_TILE, n_e), lambda i, s, c: (i, 0)),
                pl.BlockSpec((TOK_TILE, n_e), lambda i, s, c: (i, 0)),
                pl.BlockSpec((TOK_TILE, d), lambda i, s, c: (i, 0)),
                pl.BlockSpec((1, 1, d), lambda i, s, c: (i // tiles_per_seq, 0, 0)),
                pl.BlockSpec((1, d), lambda i, s, c: (0, 0)),
                pl.BlockSpec(memory_space=pl.ANY),
            ],
            out_specs=pl.BlockSpec((TOK_TILE, d), lambda i, s, c: (i, 0)),
            scratch_shapes=[
                pltpu.VMEM((2, d // LANES, _COMBINE_ROWS, LANES), _F32),
                pltpu.VMEM((TOK_TILE, d), _F32),
                pltpu.SemaphoreType.DMA((2,)),
            ],
        ),
        compiler_params=pltpu.CompilerParams(
            dimension_semantics=("arbitrary",), vmem_limit_bytes=VMEM_LIMIT),
        name="combine",
    )(start_tab, cnt_tab, pos_t, aff_tok, x1, gt2, g_final, y_exp)


def _moe(x1, h2, aff_t, gt2, g_final, w_gate, w_up, w_down, layer, final_norm):
    bsz, s, d = x1.shape
    n_tok = bsz * s
    n_e = aff_t.shape[0]
    cap = max(1, CAPACITY_FACTOR * n_tok // n_e)
    pos, start_rep = _route(aff_t, cap)
    row_start = start_rep[:, :, 0]
    row_end = jnp.concatenate([row_start[:, 1:], jnp.full((n_e, 1), cap, jnp.int32)], axis=1)
    start_tab = row_start.T.reshape(-1)
    cnt_tab = (row_end - row_start).T.reshape(-1)
    x_exp = _dispatch(h2.reshape(n_tok, d), pos, start_tab, cnt_tab, cap)
    y_exp = _ffn(x_exp, w_gate, w_up, w_down, layer)
    x2 = _combine(y_exp, pos.reshape(n_e, n_tok).T, aff_t.T, x1.reshape(n_tok, d), gt2, g_final,
                  start_tab, cnt_tab, s, final_norm)
    return x2.reshape(bsz, s, d)


def kernel(x_prompt, x_sample, c_prompt, c_sample, w_ada, b_ada, g_norm1, g_norm2, w_in, b_in, conv_w, conv_b, conv_ln_g, conv_ln_b, sgu_ln_g, sgu_ln_b, sgu_w, sgu_b, pool_w, pool_scale, w_branch, w_out, w_router, w_e_gate, w_e_up, w_e_down, g_final):
    depth = w_ada.shape[0]
    d = x_prompt.shape[-1]
    groups = ((x_prompt, c_prompt), (x_sample, c_sample))

    n_c = sum(c.shape[0] for _, c in groups)
    pad = -n_c % SUBLANES
    c_all = jnp.concatenate([c for _, c in groups] + [jnp.zeros((pad, d), _F32)], axis=0)
    mod_all = _adaln(c_all, w_ada, b_ada)

    row = lambda v: v.reshape(1, -1)
    layers = []
    for l in range(depth):
        layers.append(dict(
            g1=row(g_norm1[l]), g2=row(g_norm2[l]),
            w_in=w_in[l].astype(_BF16), b_in=row(b_in[l]),
            conv_w=conv_w[l], conv_b=row(conv_b[l]),
            conv_ln_g=row(conv_ln_g[l]), conv_ln_b=row(conv_ln_b[l]),
            sgu_ln_g=row(sgu_ln_g[l]), sgu_ln_b=row(sgu_ln_b[l]),
            sgu_w=sgu_w[l].astype(_BF16),
            sgu_b=jnp.repeat(sgu_b[l].T, d // SGU_HEADS, axis=1),
            pool_w=pool_w[l].astype(_BF16), pool_scale=row(pool_scale[l]),
            w_branch=w_branch[l].astype(_BF16), w_out=w_out[l].astype(_BF16),
            w_router_t=w_router[l].T,
        ))
    gf = row(g_final)

    outs = []
    c0 = 0
    for x, c in groups:
        bsz = x.shape[0]
        for l in range(depth):
            mod = mod_all[l, c0:c0 + bsz].reshape(bsz, 6, d)
            x1, h2, aff_t = _mixer(x, mod, layers[l])
            x = _moe(x1, h2, aff_t, mod[:, 5:6, :], gf, w_e_gate, w_e_up, w_e_down, l,
                     final_norm=(l == depth - 1))
        outs.append(x)
        c0 += bsz
    return tuple(outs)
```

```python
import functools

import jax
import jax.numpy as jnp
from jax import lax
from jax.experimental import pallas as pl
from jax.experimental.pallas import tpu as pltpu

EPS = 1e-6
CONV_WIDTH = 31
CONV_PAD = CONV_WIDTH // 2
SGU_CHUNK = 128
SGU_HEADS = 8
POOL_WINDOWS = (2, 4, 8, 16)
N_BRANCH = 3
N_EXPERTS = 16
CAPACITY_FACTOR = 2

LANES = 128
SUBLANES = 8
HALO = 16
MIX_TILE = 256
TOK_TILE = 128
DISPATCH_CHUNK = 384
COMBINE_CHUNK = 512
FFN_ROWS = 1024
FFN_COLS = 1024
FFN_SUB_ROWS = 512
VMEM_LIMIT = 56 * 1024 * 1024
NOT_SELECTED = -(1 << 20)

_F32 = jnp.float32
_BF16 = jnp.bfloat16


def _dot(a, b):
    return jnp.dot(a, b, preferred_element_type=_F32)


def _sigmoid(x):
    return 1.0 / (1.0 + jnp.exp(-x))


def _rms_mod(x, scale, shift):
    return x * lax.rsqrt(jnp.mean(x * x, axis=-1, keepdims=True) + EPS) * scale + shift


def _layernorm(x, g, b):
    mu = jnp.mean(x, axis=-1, keepdims=True)
    xc = x - mu
    var = jnp.mean(xc * xc, axis=-1, keepdims=True)
    return xc * lax.rsqrt(var + EPS) * g + b


def _adaln_kernel(c_ref, w_ref, b_ref, o_ref):
    c = c_ref[...]
    s = c * _sigmoid(c)
    o_ref[0] = jnp.dot(s, w_ref[0], preferred_element_type=_F32,
                       precision=lax.Precision.HIGHEST) + b_ref[0]


def _adaln(c_all, w_ada, b_ada):
    depth, d, n = w_ada.shape
    rows = c_all.shape[0]
    return pl.pallas_call(
        _adaln_kernel,
        out_shape=jax.ShapeDtypeStruct((depth, rows, n), _F32),
        grid=(depth, n // d),
        in_specs=[
            pl.BlockSpec((rows, d), lambda l, j: (0, 0)),
            pl.BlockSpec((1, d, d), lambda l, j: (l, 0, j)),
            pl.BlockSpec((1, 1, d), lambda l, j: (l, 0, j)),
        ],
        out_specs=pl.BlockSpec((1, rows, d), lambda l, j: (l, 0, j)),
        name="adaln",
    )(c_all, w_ada, b_ada.reshape(depth, 1, n))


def _window_sum(x, w):
    n = x.shape[0]
    a = x
    span = 1
    while span < w // 2:
        a = a + pltpu.roll(a, n - span, axis=0)
        span *= 2
    return pltpu.roll(a, w // 2, axis=0) + a


def _mixer_kernel(xp_ref, x_ref, xn_ref, mod_ref, g1_ref, g2_ref, win_ref, bin_ref,
                  convw_ref, convb_ref, clng_ref, clnb_ref, slng_ref, slnb_ref, sguw_ref, sgub_ref,
                  poolw_ref, pscale_ref, wbr_ref, wout_ref, wrh_ref, wrl_ref,
                  x1_ref, h2_ref, aff_ref,
                  hbuf, gbuf, sbuf, cbuf, ybuf, *, seq_len):
    tm = x_ref.shape[1]
    d = x_ref.shape[2]
    i = pl.program_id(1)
    n_tiles = pl.num_programs(1)

    sh1 = mod_ref[0, 0:1, :]
    sc1 = mod_ref[0, 1:2, :]
    gt1 = mod_ref[0, 2:3, :]
    sh2 = mod_ref[0, 3:4, :]
    sc2 = mod_ref[0, 4:5, :]
    scale1 = g1_ref[...] * (1.0 + sc1)
    scale2 = g2_ref[...] * (1.0 + sc2)

    xm = x_ref[0]
    hbuf[0:HALO, :] = _rms_mod(xp_ref[0], scale1, sh1).astype(_BF16)
    hbuf[HALO:HALO + tm, :] = _rms_mod(xm, scale1, sh1).astype(_BF16)
    hbuf[HALO + tm:, :] = _rms_mod(xn_ref[0], scale1, sh1).astype(_BF16)

    rows = lax.broadcasted_iota(jnp.int32, (tm + 2 * HALO, 1), 0)
    valid = jnp.logical_and(jnp.logical_or(rows >= HALO, i > 0),
                            jnp.logical_or(rows < tm + HALO, i < n_tiles - 1))

    o_u = 2 * d
    o_v = 3 * d
    o_c = 4 * d
    o_g = 5 * d

    pa = _dot(hbuf[...], win_ref[:, 0:o_u]) + bin_ref[:, 0:o_u]
    glu = pa[:, :d] * _sigmoid(pa[:, d:])
    gbuf[...] = jnp.where(valid, glu, 0.0)
    n_sh = sbuf.shape[1]
    for r in range(1, SUBLANES):
        sbuf[r - 1] = gbuf[r:r + n_sh, :]
    n_grp = 4
    rc = n_grp * SUBLANES
    for c in range(tm // rc):
        r0 = c * rc
        accs = [None] * n_grp
        for k in range(CONV_WIDTH):
            q, r = divmod(HALO - CONV_PAD + k, SUBLANES)
            w8 = convw_ref[k]
            for g in range(n_grp):
                lo = r0 + (q + g) * SUBLANES
                src = gbuf[lo:lo + SUBLANES, :] if r == 0 else sbuf[r - 1, lo:lo + SUBLANES, :]
                term = src * w8
                accs[g] = term if accs[g] is None else accs[g] + term
        acc = jnp.concatenate(accs, axis=0)
        y = _layernorm(acc + convb_ref[...], clng_ref[...], clnb_ref[...])
        ybuf[0, r0:r0 + rc, :] = (y * _sigmoid(y)).astype(_BF16)

    hm = hbuf[HALO:HALO + tm, :]
    pu = _dot(hm, win_ref[:, o_u:o_v]) + bin_ref[:, o_u:o_v]
    pv = _dot(hm, win_ref[:, o_v:o_c]) + bin_ref[:, o_v:o_c]
    vn = _layernorm(pv, slng_ref[...], slnb_ref[...]).astype(_BF16)
    dh = d // SGU_HEADS
    for n in range(tm // SGU_CHUNK):
        r0 = n * SGU_CHUNK
        for hd in range(SGU_HEADS):
            c0 = hd * dh
            vm = _dot(sguw_ref[hd], vn[r0:r0 + SGU_CHUNK, c0:c0 + dh]) + sgub_ref[:, c0:c0 + dh]
            ybuf[1, r0:r0 + SGU_CHUNK, c0:c0 + dh] = (pu[r0:r0 + SGU_CHUNK, c0:c0 + dh] * vm).astype(_BF16)

    pc = _dot(hbuf[...], win_ref[:, o_c:o_g]) + bin_ref[:, o_c:o_g]
    cbuf[...] = jnp.where(valid, pc, 0.0)
    pos = i * tm + lax.broadcasted_iota(jnp.int32, (tm, 1), 0)
    gc = d // len(POOL_WINDOWS)
    for g, w in enumerate(POOL_WINDOWS):
        c0 = g * gc
        half = w // 2
        tot = _window_sum(cbuf[:, c0:c0 + gc], w)[HALO:HALO + tm]
        lo = jnp.maximum(pos - half, 0)
        hi = jnp.minimum(pos - half + w, seq_len)
        mean = tot / (hi - lo).astype(_F32)
        diff = mean - cbuf[HALO:HALO + tm, c0:c0 + gc]
        yc = _dot(diff.astype(_BF16), poolw_ref[g]) * pscale_ref[:, c0:c0 + gc]
        ybuf[2, :, c0:c0 + gc] = yc.astype(_BF16)

    merged = jnp.zeros((tm, d), _F32)
    for b in range(N_BRANCH):
        gate = _sigmoid(_dot(hm, win_ref[:, o_g + b * d:o_g + (b + 1) * d]) + bin_ref[:, o_g + b * d:o_g + (b + 1) * d])
        merged = merged + _dot(ybuf[b], wbr_ref[b]) * gate
    mix = _dot(merged.astype(_BF16), wout_ref[...])
    x1 = xm + gt1 * mix
    x1_ref[0] = x1

    h2 = _rms_mod(x1, scale2, sh2)
    h2_hi = h2.astype(_BF16)
    h2_ref[0] = h2_hi
    h2_lo = (h2 - h2_hi.astype(_F32)).astype(_BF16)
    logits = _dot(h2_hi, wrh_ref[...]) + (_dot(h2_lo, wrh_ref[...]) + _dot(h2_hi, wrl_ref[...]))
    m = jnp.max(logits, axis=1, keepdims=True)
    ex = jnp.exp(logits - m)
    aff_ref[...] = ex / jnp.sum(ex, axis=1, keepdims=True)


def _const_spec(shape):
    nd = len(shape)
    return pl.BlockSpec(shape, lambda b, i: (0,) * nd, pipeline_mode=pl.Buffered(1))


def _mixer(x, mod, lw, tm=MIX_TILE):
    bsz, s, d = x.shape
    nt = s // tm
    hb = tm // HALO
    n_in = lw["w_in"].shape[1]
    x_halo = lambda f: pl.BlockSpec((1, HALO, d), f)
    return pl.pallas_call(
        functools.partial(_mixer_kernel, seq_len=s),
        out_shape=(
            jax.ShapeDtypeStruct((bsz, s, d), _F32),
            jax.ShapeDtypeStruct((bsz, s, d), _BF16),
            jax.ShapeDtypeStruct((bsz * s, N_EXPERTS), _F32),
        ),
        grid=(bsz, nt),
        in_specs=[
            x_halo(lambda b, i: (b, jnp.maximum(i * hb - 1, 0), 0)),
            pl.BlockSpec((1, tm, d), lambda b, i: (b, i, 0)),
            x_halo(lambda b, i: (b, jnp.minimum((i + 1) * hb, s // HALO - 1), 0)),
            pl.BlockSpec((1, 6, d), lambda b, i: (b, 0, 0)),
            _const_spec((1, d)), _const_spec((1, d)),
            _const_spec((d, n_in)), _const_spec((1, n_in)),
            _const_spec((CONV_WIDTH, SUBLANES, d)), _const_spec((1, d)), _const_spec((1, d)), _const_spec((1, d)),
            _const_spec((1, d)), _const_spec((1, d)),
            _const_spec((SGU_HEADS, SGU_CHUNK, SGU_CHUNK)), _const_spec((SGU_CHUNK, d)),
            _const_spec(lw["pool_w"].shape), _const_spec((1, d)),
            _const_spec((N_BRANCH, d, d)), _const_spec((d, d)),
            _const_spec((d, N_EXPERTS)), _const_spec((d, N_EXPERTS)),
        ],
        out_specs=(
            pl.BlockSpec((1, tm, d), lambda b, i: (b, i, 0)),
            pl.BlockSpec((1, tm, d), lambda b, i: (b, i, 0)),
            pl.BlockSpec((tm, N_EXPERTS), lambda b, i: (b * nt + i, 0)),
        ),
        scratch_shapes=[
            pltpu.VMEM((tm + 2 * HALO, d), _BF16),
            pltpu.VMEM((tm + 2 * HALO, d), _F32),
            pltpu.VMEM((SUBLANES - 1, tm + 2 * HALO - SUBLANES, d), _F32),
            pltpu.VMEM((tm + 2 * HALO, d), _F32),
            pltpu.VMEM((N_BRANCH, tm, d), _BF16),
        ],
        compiler_params=pltpu.CompilerParams(
            dimension_semantics=("arbitrary", "arbitrary"), vmem_limit_bytes=VMEM_LIMIT),
        name="mixer",
    )(x, x, x, mod, lw["g1"], lw["g2"], lw["w_in"], lw["b_in"], lw["conv_w"], lw["conv_b"],
      lw["conv_ln_g"], lw["conv_ln_b"], lw["sgu_ln_g"], lw["sgu_ln_b"], lw["sgu_w"], lw["sgu_b"],
      lw["pool_w"], lw["pool_scale"], lw["w_branch"], lw["w_out"], lw["w_router_hi"], lw["w_router_lo"])


def _route_kernel(aff_ref, pos_ref, start_ref, *, cap, n_tok):
    n_e, n_rows, _ = aff_ref.shape
    keys = pltpu.bitcast(aff_ref[...], jnp.int32)

    def count(ind):
        return jnp.sum(jnp.sum(ind, axis=1, keepdims=True), axis=2, keepdims=True)

    def thr_step(it, prefix):
        cand = prefix | jnp.left_shift(jnp.int32(1), 30 - it)
        c = count(jnp.where(keys >= cand, 1.0, 0.0))
        return jnp.where(c >= cap, cand, prefix)

    thr = lax.fori_loop(0, 31, thr_step, jnp.zeros((n_e, 1, 1), jnp.int32))
    gt = keys > thr
    eq = keys == thr
    need = cap - count(jnp.where(gt, 1.0, 0.0))
    tidx = (lax.broadcasted_iota(jnp.int32, (1, n_rows, LANES), 1) * LANES
            + lax.broadcasted_iota(jnp.int32, (1, n_rows, LANES), 2))
    n_bits = max(1, (n_tok - 1).bit_length())

    def tie_step(it, bound):
        cand = bound + jnp.left_shift(jnp.int32(1), n_bits - 1 - it)
        c = count(jnp.where(eq, jnp.where(tidx < cand, 1.0, 0.0), 0.0))
        return jnp.where(c < need, cand, bound)

    bound = lax.fori_loop(0, n_bits, tie_step, jnp.zeros((n_e, 1, 1), jnp.int32))
    sel = jnp.where(gt, 1.0, jnp.where(eq, jnp.where(tidx <= bound, 1.0, 0.0), 0.0))

    r_i = lax.broadcasted_iota(jnp.int32, (LANES, LANES), 0)
    c_i = lax.broadcasted_iota(jnp.int32, (LANES, LANES), 1)
    upper = jnp.where(r_i <= c_i, 1.0, 0.0).astype(_BF16)
    ones = jnp.ones((LANES, LANES), _BF16)
    rr = lax.broadcasted_iota(jnp.int32, (n_rows, n_rows), 0)
    rc = lax.broadcasted_iota(jnp.int32, (n_rows, n_rows), 1)
    lower = jnp.where(rc < rr, 1.0, 0.0).astype(_BF16)
    for e in range(n_e):
        s = sel[e]
        sb = s.astype(_BF16)
        incl = _dot(sb, upper)
        tot = _dot(sb, ones)
        start = _dot(lower, tot.astype(_BF16))
        slot = (start + incl).astype(jnp.int32) - 1
        pos_ref[e] = jnp.where(s > 0.5, slot, NOT_SELECTED)
        start_ref[e] = start.astype(jnp.int32)


def _route(aff_t, cap):
    n_e, n_tok = aff_t.shape
    n_rows = n_tok // LANES
    return pl.pallas_call(
        functools.partial(_route_kernel, cap=cap, n_tok=n_tok),
        out_shape=(
            jax.ShapeDtypeStruct((n_e, n_rows, LANES), jnp.int32),
            jax.ShapeDtypeStruct((n_e, n_rows, LANES), jnp.int32),
        ),
        name="route",
    )(aff_t.reshape(n_e, n_rows, LANES))


def _round_up(n, m):
    return -(-n // m) * m


_DISPATCH_ROWS = _round_up(N_EXPERTS * TOK_TILE, DISPATCH_CHUNK)


def _dispatch_copies(start_ref, cnt_ref, xbuf, x_hbm, sems, tile, slot, act):
    run = 0
    for e in range(N_EXPERTS):
        cnt = cnt_ref[tile * N_EXPERTS + e]
        st = start_ref[tile * N_EXPERTS + e]

        @pl.when(cnt > 0)
        def _(run=run, cnt=cnt, st=st, e=e):
            src = xbuf.at[slot, pl.ds(pl.multiple_of(run * SUBLANES, SUBLANES), cnt * SUBLANES)]
            dst = x_hbm.at[e, pl.ds(pl.multiple_of(st * SUBLANES, SUBLANES), cnt * SUBLANES)]
            act(pltpu.make_async_copy(src, dst, sems.at[slot]))
        run = run + cnt


def _dispatch_kernel(start_ref, cnt_ref, pos_ref, h2_ref, x_hbm, xbuf, sems):
    i = pl.program_id(0)
    last = pl.num_programs(0) - 1
    slot = i % 2
    tt, d = h2_ref.shape

    shift = []
    run = 0
    for e in range(N_EXPERTS):
        shift.append(run - start_ref[i * N_EXPERTS + e])
        run = run + cnt_ref[i * N_EXPERTS + e]
    n_rows = run

    for k in range(_DISPATCH_ROWS // DISPATCH_CHUNK):
        @pl.when(k * DISPATCH_CHUNK < n_rows)
        def _(k=k):
            c_i = k * DISPATCH_CHUNK + lax.broadcasted_iota(jnp.int32, (DISPATCH_CHUNK, tt), 0)
            pt = jnp.zeros((DISPATCH_CHUNK, tt), _F32)
            for e in range(N_EXPERTS):
                pt = jnp.where(c_i == pos_ref[e, 0] + shift[e], 1.0, pt)
            ptb = pt.astype(_BF16)
            for j2 in range(d // (2 * LANES)):
                r = _dot(ptb, h2_ref[:, j2 * 2 * LANES:(j2 + 1) * 2 * LANES])
                for h in range(2):
                    rows = pl.ds(k * DISPATCH_CHUNK * SUBLANES + 2 * j2 + h, DISPATCH_CHUNK, stride=SUBLANES)
                    xbuf[slot, rows, :] = r[:, h * LANES:(h + 1) * LANES]

    @pl.when(i > 0)
    def _():
        _dispatch_copies(start_ref, cnt_ref, xbuf, x_hbm, sems, i - 1, 1 - slot, lambda cp: cp.wait())

    _dispatch_copies(start_ref, cnt_ref, xbuf, x_hbm, sems, i, slot, lambda cp: cp.start())

    @pl.when(i == last)
    def _():
        _dispatch_copies(start_ref, cnt_ref, xbuf, x_hbm, sems, i, slot, lambda cp: cp.wait())


def _dispatch(h2, pos, start_tab, cnt_tab, cap):
    n_tok, d = h2.shape
    n_e, n_rows, _ = pos.shape
    assert d == SUBLANES * LANES, "one activation row must fill one 8 x 128 tile"
    n_tiles = n_tok // TOK_TILE
    return pl.pallas_call(
        _dispatch_kernel,
        out_shape=jax.ShapeDtypeStruct((n_e, cap * SUBLANES, LANES), _F32),
        grid_spec=pltpu.PrefetchScalarGridSpec(
            num_scalar_prefetch=2,
            grid=(n_tiles,),
            in_specs=[
                pl.BlockSpec((n_e, 1, 1, LANES), lambda i, s, c: (0, i, 0, 0)),
                pl.BlockSpec((TOK_TILE, d), lambda i, s, c: (i, 0)),
            ],
            out_specs=pl.BlockSpec(memory_space=pl.ANY),
            scratch_shapes=[
                pltpu.VMEM((2, _DISPATCH_ROWS * SUBLANES, LANES), _F32),
                pltpu.SemaphoreType.DMA((2,)),
            ],
        ),
        compiler_params=pltpu.CompilerParams(
            dimension_semantics=("arbitrary",), vmem_limit_bytes=VMEM_LIMIT),
        name="dispatch",
    )(start_tab, cnt_tab, pos.reshape(n_e, n_rows, 1, LANES), h2)


def _ffn_kernel(x_ref, wg_ref, wu_ref, wd_ref, y_ref, xb):
    f = pl.program_id(2)
    n_j = SUBLANES
    mt = xb.shape[0]

    @pl.when(f == 0)
    def _():
        for j in range(n_j):
            xb[:, j * LANES:(j + 1) * LANES] = x_ref[0, pl.ds(j, mt, stride=SUBLANES), :].astype(_BF16)
        y_ref[...] = jnp.zeros_like(y_ref)

    wg = wg_ref[0, 0].astype(_BF16)
    wu = wu_ref[0, 0].astype(_BF16)
    wd = wd_ref[0, 0].astype(_BF16)
    rb = min(FFN_SUB_ROWS, mt)
    for s in range(mt // rb):
        x = xb[s * rb:(s + 1) * rb, :]
        g = _dot(x, wg)
        u = _dot(x, wu)
        hid = (g * _sigmoid(g) * u).astype(_BF16)
        res = _dot(hid, wd)
        for j in range(n_j):
            y_ref[0, j, s * rb:(s + 1) * rb, :] += res[:, j * LANES:(j + 1) * LANES]


def _ffn(x_exp, w_gate, w_up, w_down, layer):
    n_e = x_exp.shape[0]
    n_j = SUBLANES
    cap = x_exp.shape[1] // n_j
    d = n_j * LANES
    d_exp = w_gate.shape[-1]
    mt = min(FFN_ROWS, cap)
    fc = min(FFN_COLS, d_exp)
    return pl.pallas_call(
        _ffn_kernel,
        out_shape=jax.ShapeDtypeStruct((n_e, n_j, cap, LANES), _F32),
        grid=(n_e, cap // mt, d_exp // fc),
        in_specs=[
            pl.BlockSpec((1, mt * n_j, LANES), lambda e, m, f: (e, m, 0)),
            pl.BlockSpec((1, 1, d, fc), lambda e, m, f: (layer, e, 0, f)),
            pl.BlockSpec((1, 1, d, fc), lambda e, m, f: (layer, e, 0, f)),
            pl.BlockSpec((1, 1, fc, d), lambda e, m, f: (layer, e, f, 0)),
        ],
        out_specs=pl.BlockSpec((1, n_j, mt, LANES), lambda e, m, f: (e, 0, m, 0)),
        scratch_shapes=[pltpu.VMEM((mt, d), _BF16)],
        compiler_params=pltpu.CompilerParams(
            dimension_semantics=("arbitrary", "arbitrary", "arbitrary"), vmem_limit_bytes=VMEM_LIMIT),
        name="ffn",
    )(x_exp, w_gate, w_up, w_down)


_COMBINE_ROWS = _round_up(N_EXPERTS * (TOK_TILE + 2 * SUBLANES), COMBINE_CHUNK)
_COL_BITS = 6
_COL_SPLIT = 1 << _COL_BITS
_NO_COLUMN = _COL_SPLIT * _COL_SPLIT - 1
assert _COMBINE_ROWS <= _NO_COLUMN


def _combine_windows(start_ref, cnt_ref, tile):
    out = []
    run = 0
    for e in range(N_EXPERTS):
        cnt = cnt_ref[tile * N_EXPERTS + e]
        st = start_ref[tile * N_EXPERTS + e]
        a0 = (st >> 3) << 3
        a1 = ((st + cnt + SUBLANES - 1) >> 3) << 3
        rows = jnp.where(cnt > 0, a1 - a0, 0)
        out.append((a0, rows, run))
        run = run + rows
    return out, run


def _combine_copies(start_ref, cnt_ref, y_hbm, ybuf, sems, tile, slot, act):
    wins, _ = _combine_windows(start_ref, cnt_ref, tile)
    for e, (a0, rows, boff) in enumerate(wins):
        @pl.when(rows > 0)
        def _(a0=a0, rows=rows, boff=boff, e=e):
            n = pl.multiple_of(rows, SUBLANES)
            src = y_hbm.at[e, :, pl.ds(pl.multiple_of(a0, SUBLANES), n), :]
            dst = ybuf.at[slot, :, pl.ds(pl.multiple_of(boff, SUBLANES), n), :]
            act(pltpu.make_async_copy(src, dst, sems.at[slot]))


def _combine_kernel(start_ref, cnt_ref, pos_ref, aff_ref, x1_ref, gt_ref, gf_ref, y_hbm, o_ref,
                    ybuf, acc, sems, *, final_norm):
    i = pl.program_id(0)
    n_tiles = pl.num_programs(0)
    slot = i % 2
    tt, d = x1_ref.shape

    @pl.when(i == 0)
    def _():
        ybuf[...] = jnp.zeros_like(ybuf)
        _combine_copies(start_ref, cnt_ref, y_hbm, ybuf, sems, i, slot, lambda cp: cp.start())

    @pl.when(i + 1 < n_tiles)
    def _():
        _combine_copies(start_ref, cnt_ref, y_hbm, ybuf, sems, i + 1, 1 - slot, lambda cp: cp.start())

    _combine_copies(start_ref, cnt_ref, y_hbm, ybuf, sems, i, slot, lambda cp: cp.wait())

    wins, n_rows = _combine_windows(start_ref, cnt_ref, i)
    n_e = pos_ref.shape[1]
    ch = COMBINE_CHUNK

    lane_e = lax.broadcasted_iota(jnp.int32, (1, n_e), 1)
    shift = jnp.zeros((1, n_e), jnp.int32)
    for e, (a0, _, boff) in enumerate(wins):
        shift = jnp.where(lane_e == e, boff - a0, shift)
    pos = pos_ref[...]
    col = jnp.where(pos >= 0, pos + shift, _NO_COLUMN)
    col_hi = (col >> _COL_BITS).astype(_F32).astype(_BF16)
    col_lo = (col & (_COL_SPLIT - 1)).astype(_F32).astype(_BF16)
    aff = aff_ref[...]
    a_hi = aff.astype(_BF16)
    a_lo = (aff - a_hi.astype(_F32)).astype(_BF16)

    acc[...] = jnp.zeros_like(acc)
    for k in range(_COMBINE_ROWS // ch):
        @pl.when(k * ch < n_rows)
        def _(k=k):
            c_i = k * ch + lax.broadcasted_iota(jnp.int32, (1, ch), 1)
            owner = jnp.zeros((1, ch), jnp.int32)
            for _, rows, boff in wins:
                owner = owner + jnp.where(c_i >= boff + rows, 1, 0)
            spread = jnp.where(lax.broadcasted_iota(jnp.int32, (n_e, ch), 0) == owner, 1.0, 0.0).astype(_BF16)
            col_c = _COL_SPLIT * _dot(col_hi, spread) + _dot(col_lo, spread)
            hit = col_c == c_i.astype(_F32)
            ph = jnp.where(hit, _dot(a_hi, spread), 0.0).astype(_BF16)
            plo = jnp.where(hit, _dot(a_lo, spread), 0.0).astype(_BF16)
            for j2 in range(d // (2 * LANES)):
                y = jnp.concatenate([ybuf[slot, 2 * j2, pl.ds(k * ch, ch), :],
                                     ybuf[slot, 2 * j2 + 1, pl.ds(k * ch, ch), :]], axis=1)
                yh = y.astype(_BF16)
                ylo = (y - yh.astype(_F32)).astype(_BF16)
                acc[:, j2 * 2 * LANES:(j2 + 1) * 2 * LANES] += _dot(ph, yh) + (_dot(ph, ylo) + _dot(plo, yh))

    x2 = x1_ref[...] + gt_ref[0] * acc[...]
    if final_norm:
        x2 = x2 * lax.rsqrt(jnp.mean(x2 * x2, axis=-1, keepdims=True) + EPS) * gf_ref[...]
    o_ref[...] = x2


def _combine(y_exp, pos_t, aff_tok, x1, gt2, g_final, start_tab, cnt_tab, seq_len, final_norm):
    n_tok, d = x1.shape
    n_e = y_exp.shape[0]
    n_tiles = n_tok // TOK_TILE
    tiles_per_seq = seq_len // TOK_TILE
    return pl.pallas_call(
        functools.partial(_combine_kernel, final_norm=final_norm),
        out_shape=jax.ShapeDtypeStruct((n_tok, d), _F32),
        grid_spec=pltpu.PrefetchScalarGridSpec(
            num_scalar_prefetch=2,
            grid=(n_tiles,),
            in_specs=[
                pl.BlockSpec((TOK_TILE, n_e), lambda i, s, c: (i, 0)),
                pl.BlockSpec((TOK_TILE, n_e), lambda i, s, c: (i, 0)),
                pl.BlockSpec((TOK_TILE, d), lambda i, s, c: (i, 0)),
                pl.BlockSpec((1, 1, d), lambda i, s, c: (i // tiles_per_seq, 0, 0)),
                pl.BlockSpec((1, d), lambda i, s, c: (0, 0)),
                pl.BlockSpec(memory_space=pl.ANY),
            ],
            out_specs=pl.BlockSpec((TOK_TILE, d), lambda i, s, c: (i, 0)),
            scratch_shapes=[
                pltpu.VMEM((2, d // LANES, _COMBINE_ROWS, LANES), _F32),
                pltpu.VMEM((TOK_TILE, d), _F32),
                pltpu.SemaphoreType.DMA((2,)),
            ],
        ),
        compiler_params=pltpu.CompilerParams(
            dimension_semantics=("arbitrary",), vmem_limit_bytes=VMEM_LIMIT),
        name="combine",
    )(start_tab, cnt_tab, pos_t, aff_tok, x1, gt2, g_final, y_exp)


def _moe(x1, h2, aff_tok, gt2, g_final, w_gate, w_up, w_down, layer, final_norm):
    bsz, s, d = x1.shape
    n_tok, n_e = aff_tok.shape
    cap = max(1, CAPACITY_FACTOR * n_tok // n_e)
    pos, start_rep = _route(aff_tok.T, cap)
    row_start = start_rep[:, :, 0]
    row_end = jnp.concatenate([row_start[:, 1:], jnp.full((n_e, 1), cap, jnp.int32)], axis=1)
    start_tab = row_start.T.reshape(-1)
    cnt_tab = (row_end - row_start).T.reshape(-1)
    x_exp = _dispatch(h2.reshape(n_tok, d), pos, start_tab, cnt_tab, cap)
    y_exp = _ffn(x_exp, w_gate, w_up, w_down, layer)
    x2 = _combine(y_exp, pos.reshape(n_e, n_tok).T, aff_tok, x1.reshape(n_tok, d), gt2, g_final,
                  start_tab, cnt_tab, s, final_norm)
    return x2.reshape(bsz, s, d)


def kernel(x_prompt, x_sample, c_prompt, c_sample, w_ada, b_ada, g_norm1, g_norm2, w_in, b_in, conv_w, conv_b, conv_ln_g, conv_ln_b, sgu_ln_g, sgu_ln_b, sgu_w, sgu_b, pool_w, pool_scale, w_branch, w_out, w_router, w_e_gate, w_e_up, w_e_down, g_final):
    depth = w_ada.shape[0]
    d = x_prompt.shape[-1]
    groups = ((x_prompt, c_prompt), (x_sample, c_sample))

    n_c = sum(c.shape[0] for _, c in groups)
    pad = -n_c % SUBLANES
    c_all = jnp.concatenate([c for _, c in groups] + [jnp.zeros((pad, d), _F32)], axis=0)
    mod_all = _adaln(c_all, w_ada, b_ada)

    row = lambda v: v.reshape(1, -1)
    layers = []
    for l in range(depth):
        wr_hi = w_router[l].astype(_BF16)
        layers.append(dict(
            g1=row(g_norm1[l]), g2=row(g_norm2[l]),
            w_in=w_in[l].astype(_BF16), b_in=row(b_in[l]),
            conv_w=jnp.broadcast_to(conv_w[l][:, None, :], (CONV_WIDTH, SUBLANES, d)), conv_b=row(conv_b[l]),
            conv_ln_g=row(conv_ln_g[l]), conv_ln_b=row(conv_ln_b[l]),
            sgu_ln_g=row(sgu_ln_g[l]), sgu_ln_b=row(sgu_ln_b[l]),
            sgu_w=sgu_w[l].astype(_BF16),
            sgu_b=jnp.repeat(sgu_b[l].T, d // SGU_HEADS, axis=1),
            pool_w=pool_w[l].astype(_BF16), pool_scale=row(pool_scale[l]),
            w_branch=w_branch[l].astype(_BF16), w_out=w_out[l].astype(_BF16),
            w_router_hi=wr_hi, w_router_lo=(w_router[l] - wr_hi.astype(_F32)).astype(_BF16),
        ))
    gf = row(g_final)

    outs = []
    c0 = 0
    for x, c in groups:
        bsz = x.shape[0]
        for l in range(depth):
            mod = mod_all[l, c0:c0 + bsz].reshape(bsz, 6, d)
            x1, h2, aff_tok = _mixer(x, mod, layers[l])
            x = _moe(x1, h2, aff_tok, mod[:, 5:6, :], gf, w_e_gate, w_e_up, w_e_down, l,
                     final_norm=(l == depth - 1))
        outs.append(x)
        c0 += bsz
    return tuple(outs)
```

```python
import functools

import jax
import jax.numpy as jnp
from jax import lax
from jax.experimental import pallas as pl
from jax.experimental.pallas import tpu as pltpu

EPS = 1e-6
CONV_WIDTH = 31
CONV_PAD = CONV_WIDTH // 2
SGU_CHUNK = 128
SGU_HEADS = 8
POOL_WINDOWS = (2, 4, 8, 16)
N_BRANCH = 3
N_EXPERTS = 16
CAPACITY_FACTOR = 2

LANES = 128
SUBLANES = 8
HALO = 16
MIX_TILE = 256
DISPATCH_TILE = 128
COMBINE_TILE = 256
DISPATCH_CHUNK = 384
COMBINE_CHUNK = 768
FFN_ROWS = 1024
FFN_COLS = 1024
FFN_SUB_ROWS = 512
VMEM_LIMIT = 56 * 1024 * 1024
NOT_SELECTED = -(1 << 20)

_F32 = jnp.float32
_BF16 = jnp.bfloat16


def _dot(a, b):
    return jnp.dot(a, b, preferred_element_type=_F32)


def _sigmoid(x):
    return 1.0 / (1.0 + jnp.exp(-x))


def _rms_mod(x, scale, shift):
    return x * lax.rsqrt(jnp.mean(x * x, axis=-1, keepdims=True) + EPS) * scale + shift


def _layernorm(x, g, b):
    mu = jnp.mean(x, axis=-1, keepdims=True)
    xc = x - mu
    var = jnp.mean(xc * xc, axis=-1, keepdims=True)
    return xc * lax.rsqrt(var + EPS) * g + b


def _adaln_kernel(c_ref, w_ref, b_ref, o_ref):
    c = c_ref[...]
    s = c * _sigmoid(c)
    o_ref[0] = jnp.dot(s, w_ref[0], preferred_element_type=_F32,
                       precision=lax.Precision.HIGHEST) + b_ref[0]


def _adaln(c_all, w_ada, b_ada):
    depth, d, n = w_ada.shape
    rows = c_all.shape[0]
    return pl.pallas_call(
        _adaln_kernel,
        out_shape=jax.ShapeDtypeStruct((depth, rows, n), _F32),
        grid=(depth, n // d),
        in_specs=[
            pl.BlockSpec((rows, d), lambda l, j: (0, 0)),
            pl.BlockSpec((1, d, d), lambda l, j: (l, 0, j)),
            pl.BlockSpec((1, 1, d), lambda l, j: (l, 0, j)),
        ],
        out_specs=pl.BlockSpec((1, rows, d), lambda l, j: (l, 0, j)),
        name="adaln",
    )(c_all, w_ada, b_ada.reshape(depth, 1, n))


def _window_sum(x, w):
    n = x.shape[0]
    a = x
    span = 1
    while span < w // 2:
        a = a + pltpu.roll(a, n - span, axis=0)
        span *= 2
    return pltpu.roll(a, w // 2, axis=0) + a


def _mixer_kernel(xp_ref, x_ref, xn_ref, mod_ref, g1_ref, g2_ref, win_ref, bin_ref,
                  convw_ref, convb_ref, clng_ref, clnb_ref, slng_ref, slnb_ref, sguw_ref, sgub_ref,
                  poolw_ref, pscale_ref, wbr_ref, wout_ref, wrh_ref, wrl_ref,
                  x1_ref, h2_ref, aff_ref,
                  hbuf, gbuf, sbuf, cbuf, ybuf, *, seq_len):
    tm = x_ref.shape[1]
    d = x_ref.shape[2]
    i = pl.program_id(1)
    n_tiles = pl.num_programs(1)

    sh1 = mod_ref[0, 0:1, :]
    sc1 = mod_ref[0, 1:2, :]
    gt1 = mod_ref[0, 2:3, :]
    sh2 = mod_ref[0, 3:4, :]
    sc2 = mod_ref[0, 4:5, :]
    scale1 = g1_ref[...] * (1.0 + sc1)
    scale2 = g2_ref[...] * (1.0 + sc2)

    xm = x_ref[0]
    hbuf[0:HALO, :] = _rms_mod(xp_ref[0], scale1, sh1).astype(_BF16)
    hbuf[HALO:HALO + tm, :] = _rms_mod(xm, scale1, sh1).astype(_BF16)
    hbuf[HALO + tm:, :] = _rms_mod(xn_ref[0], scale1, sh1).astype(_BF16)

    rows = lax.broadcasted_iota(jnp.int32, (tm + 2 * HALO, 1), 0)
    valid = jnp.logical_and(jnp.logical_or(rows >= HALO, i > 0),
                            jnp.logical_or(rows < tm + HALO, i < n_tiles - 1))

    o_u = 2 * d
    o_v = 3 * d
    o_c = 4 * d
    o_g = 5 * d

    pa = _dot(hbuf[...], win_ref[:, 0:o_u]) + bin_ref[:, 0:o_u]
    glu = pa[:, :d] * _sigmoid(pa[:, d:])
    gbuf[...] = jnp.where(valid, glu, 0.0)
    n_sh = sbuf.shape[1]
    for r in range(1, SUBLANES):
        sbuf[r - 1] = gbuf[r:r + n_sh, :]
    n_grp = 4
    rc = n_grp * SUBLANES
    for c in range(tm // rc):
        r0 = c * rc
        accs = [None] * n_grp
        for k in range(CONV_WIDTH):
            q, r = divmod(HALO - CONV_PAD + k, SUBLANES)
            w8 = convw_ref[k]
            for g in range(n_grp):
                lo = r0 + (q + g) * SUBLANES
                src = gbuf[lo:lo + SUBLANES, :] if r == 0 else sbuf[r - 1, lo:lo + SUBLANES, :]
                term = src * w8
                accs[g] = term if accs[g] is None else accs[g] + term
        acc = jnp.concatenate(accs, axis=0)
        y = _layernorm(acc + convb_ref[...], clng_ref[...], clnb_ref[...])
        ybuf[0, r0:r0 + rc, :] = (y * _sigmoid(y)).astype(_BF16)

    hm = hbuf[HALO:HALO + tm, :]
    pu = _dot(hm, win_ref[:, o_u:o_v]) + bin_ref[:, o_u:o_v]
    pv = _dot(hm, win_ref[:, o_v:o_c]) + bin_ref[:, o_v:o_c]
    vn = _layernorm(pv, slng_ref[...], slnb_ref[...]).astype(_BF16)
    dh = d // SGU_HEADS
    for n in range(tm // SGU_CHUNK):
        r0 = n * SGU_CHUNK
        for hd in range(SGU_HEADS):
            c0 = hd * dh
            vm = _dot(sguw_ref[hd], vn[r0:r0 + SGU_CHUNK, c0:c0 + dh]) + sgub_ref[:, c0:c0 + dh]
            ybuf[1, r0:r0 + SGU_CHUNK, c0:c0 + dh] = (pu[r0:r0 + SGU_CHUNK, c0:c0 + dh] * vm).astype(_BF16)

    pc = _dot(hbuf[...], win_ref[:, o_c:o_g]) + bin_ref[:, o_c:o_g]
    cbuf[...] = jnp.where(valid, pc, 0.0)
    pos = i * tm + lax.broadcasted_iota(jnp.int32, (tm, 1), 0)
    gc = d // len(POOL_WINDOWS)
    for g, w in enumerate(POOL_WINDOWS):
        c0 = g * gc
        half = w // 2
        tot = _window_sum(cbuf[:, c0:c0 + gc], w)[HALO:HALO + tm]
        lo = jnp.maximum(pos - half, 0)
        hi = jnp.minimum(pos - half + w, seq_len)
        mean = tot / (hi - lo).astype(_F32)
        diff = mean - cbuf[HALO:HALO + tm, c0:c0 + gc]
        yc = _dot(diff.astype(_BF16), poolw_ref[g]) * pscale_ref[:, c0:c0 + gc]
        ybuf[2, :, c0:c0 + gc] = yc.astype(_BF16)

    merged = jnp.zeros((tm, d), _F32)
    for b in range(N_BRANCH):
        gate = _sigmoid(_dot(hm, win_ref[:, o_g + b * d:o_g + (b + 1) * d]) + bin_ref[:, o_g + b * d:o_g + (b + 1) * d])
        merged = merged + _dot(ybuf[b], wbr_ref[b]) * gate
    mix = _dot(merged.astype(_BF16), wout_ref[...])
    x1 = xm + gt1 * mix
    x1_ref[0] = x1

    h2 = _rms_mod(x1, scale2, sh2)
    h2_hi = h2.astype(_BF16)
    h2_ref[0] = h2_hi
    h2_lo = (h2 - h2_hi.astype(_F32)).astype(_BF16)
    logits = _dot(h2_hi, wrh_ref[...]) + (_dot(h2_lo, wrh_ref[...]) + _dot(h2_hi, wrl_ref[...]))
    m = jnp.max(logits, axis=1, keepdims=True)
    ex = jnp.exp(logits - m)
    aff_ref[...] = ex / jnp.sum(ex, axis=1, keepdims=True)


def _const_spec(shape):
    nd = len(shape)
    return pl.BlockSpec(shape, lambda b, i: (0,) * nd, pipeline_mode=pl.Buffered(1))


def _mixer(x, mod, lw, tm=MIX_TILE):
    bsz, s, d = x.shape
    nt = s // tm
    hb = tm // HALO
    n_in = lw["w_in"].shape[1]
    x_halo = lambda f: pl.BlockSpec((1, HALO, d), f)
    return pl.pallas_call(
        functools.partial(_mixer_kernel, seq_len=s),
        out_shape=(
            jax.ShapeDtypeStruct((bsz, s, d), _F32),
            jax.ShapeDtypeStruct((bsz, s, d), _BF16),
            jax.ShapeDtypeStruct((bsz * s, N_EXPERTS), _F32),
        ),
        grid=(bsz, nt),
        in_specs=[
            x_halo(lambda b, i: (b, jnp.maximum(i * hb - 1, 0), 0)),
            pl.BlockSpec((1, tm, d), lambda b, i: (b, i, 0)),
            x_halo(lambda b, i: (b, jnp.minimum((i + 1) * hb, s // HALO - 1), 0)),
            pl.BlockSpec((1, 6, d), lambda b, i: (b, 0, 0)),
            _const_spec((1, d)), _const_spec((1, d)),
            _const_spec((d, n_in)), _const_spec((1, n_in)),
            _const_spec((CONV_WIDTH, SUBLANES, d)), _const_spec((1, d)), _const_spec((1, d)), _const_spec((1, d)),
            _const_spec((1, d)), _const_spec((1, d)),
            _const_spec((SGU_HEADS, SGU_CHUNK, SGU_CHUNK)), _const_spec((SGU_CHUNK, d)),
            _const_spec(lw["pool_w"].shape), _const_spec((1, d)),
            _const_spec((N_BRANCH, d, d)), _const_spec((d, d)),
            _const_spec((d, N_EXPERTS)), _const_spec((d, N_EXPERTS)),
        ],
        out_specs=(
            pl.BlockSpec((1, tm, d), lambda b, i: (b, i, 0)),
            pl.BlockSpec((1, tm, d), lambda b, i: (b, i, 0)),
            pl.BlockSpec((tm, N_EXPERTS), lambda b, i: (b * nt + i, 0)),
        ),
        scratch_shapes=[
            pltpu.VMEM((tm + 2 * HALO, d), _BF16),
            pltpu.VMEM((tm + 2 * HALO, d), _F32),
            pltpu.VMEM((SUBLANES - 1, tm + 2 * HALO - SUBLANES, d), _F32),
            pltpu.VMEM((tm + 2 * HALO, d), _F32),
            pltpu.VMEM((N_BRANCH, tm, d), _BF16),
        ],
        compiler_params=pltpu.CompilerParams(
            dimension_semantics=("arbitrary", "arbitrary"), vmem_limit_bytes=VMEM_LIMIT),
        name="mixer",
    )(x, x, x, mod, lw["g1"], lw["g2"], lw["w_in"], lw["b_in"], lw["conv_w"], lw["conv_b"],
      lw["conv_ln_g"], lw["conv_ln_b"], lw["sgu_ln_g"], lw["sgu_ln_b"], lw["sgu_w"], lw["sgu_b"],
      lw["pool_w"], lw["pool_scale"], lw["w_branch"], lw["w_out"], lw["w_router_hi"], lw["w_router_lo"])


def _route_kernel(aff_ref, pos_ref, start_ref, *, cap, n_tok):
    n_e, n_rows, _ = aff_ref.shape
    keys = pltpu.bitcast(aff_ref[...], jnp.int32)

    def count(ind):
        return jnp.sum(jnp.sum(ind, axis=1, keepdims=True), axis=2, keepdims=True)

    def thr_step(it, prefix):
        cand = prefix | jnp.left_shift(jnp.int32(1), 30 - it)
        c = count(jnp.where(keys >= cand, 1.0, 0.0))
        return jnp.where(c >= cap, cand, prefix)

    thr = lax.fori_loop(0, 31, thr_step, jnp.zeros((n_e, 1, 1), jnp.int32))
    gt = keys > thr
    eq = keys == thr
    need = cap - count(jnp.where(gt, 1.0, 0.0))
    tidx = (lax.broadcasted_iota(jnp.int32, (1, n_rows, LANES), 1) * LANES
            + lax.broadcasted_iota(jnp.int32, (1, n_rows, LANES), 2))
    n_bits = max(1, (n_tok - 1).bit_length())

    def tie_step(it, bound):
        cand = bound + jnp.left_shift(jnp.int32(1), n_bits - 1 - it)
        c = count(jnp.where(eq, jnp.where(tidx < cand, 1.0, 0.0), 0.0))
        return jnp.where(c < need, cand, bound)

    bound = lax.fori_loop(0, n_bits, tie_step, jnp.zeros((n_e, 1, 1), jnp.int32))
    sel = jnp.where(gt, 1.0, jnp.where(eq, jnp.where(tidx <= bound, 1.0, 0.0), 0.0))

    r_i = lax.broadcasted_iota(jnp.int32, (LANES, LANES), 0)
    c_i = lax.broadcasted_iota(jnp.int32, (LANES, LANES), 1)
    upper = jnp.where(r_i <= c_i, 1.0, 0.0).astype(_BF16)
    ones = jnp.ones((LANES, LANES), _BF16)
    rr = lax.broadcasted_iota(jnp.int32, (n_rows, n_rows), 0)
    rc = lax.broadcasted_iota(jnp.int32, (n_rows, n_rows), 1)
    lower = jnp.where(rc < rr, 1.0, 0.0).astype(_BF16)
    for e in range(n_e):
        s = sel[e]
        sb = s.astype(_BF16)
        incl = _dot(sb, upper)
        tot = _dot(sb, ones)
        start = _dot(lower, tot.astype(_BF16))
        slot = (start + incl).astype(jnp.int32) - 1
        pos_ref[e] = jnp.where(s > 0.5, slot, NOT_SELECTED)
        start_ref[e] = start.astype(jnp.int32)


def _route(aff_t, cap):
    n_e, n_tok = aff_t.shape
    n_rows = n_tok // LANES
    return pl.pallas_call(
        functools.partial(_route_kernel, cap=cap, n_tok=n_tok),
        out_shape=(
            jax.ShapeDtypeStruct((n_e, n_rows, LANES), jnp.int32),
            jax.ShapeDtypeStruct((n_e, n_rows, LANES), jnp.int32),
        ),
        name="route",
    )(aff_t.reshape(n_e, n_rows, LANES))


def _round_up(n, m):
    return -(-n // m) * m


_DISPATCH_ROWS = _round_up(N_EXPERTS * DISPATCH_TILE, DISPATCH_CHUNK)


def _dispatch_copies(start_ref, cnt_ref, xbuf, x_hbm, sems, tile, slot, act):
    run = 0
    for e in range(N_EXPERTS):
        cnt = cnt_ref[tile * N_EXPERTS + e]
        st = start_ref[tile * N_EXPERTS + e]

        @pl.when(cnt > 0)
        def _(run=run, cnt=cnt, st=st, e=e):
            src = xbuf.at[slot, pl.ds(pl.multiple_of(run * SUBLANES, SUBLANES), cnt * SUBLANES)]
            dst = x_hbm.at[e, pl.ds(pl.multiple_of(st * SUBLANES, SUBLANES), cnt * SUBLANES)]
            act(pltpu.make_async_copy(src, dst, sems.at[slot]))
        run = run + cnt


def _dispatch_kernel(start_ref, cnt_ref, pos_ref, h2_ref, x_hbm, xbuf, sems):
    i = pl.program_id(0)
    last = pl.num_programs(0) - 1
    slot = i % 2
    tt, d = h2_ref.shape

    shift = []
    run = 0
    for e in range(N_EXPERTS):
        shift.append(run - start_ref[i * N_EXPERTS + e])
        run = run + cnt_ref[i * N_EXPERTS + e]
    n_rows = run

    for k in range(_DISPATCH_ROWS // DISPATCH_CHUNK):
        @pl.when(k * DISPATCH_CHUNK < n_rows)
        def _(k=k):
            c_i = k * DISPATCH_CHUNK + lax.broadcasted_iota(jnp.int32, (DISPATCH_CHUNK, tt), 0)
            pt = jnp.zeros((DISPATCH_CHUNK, tt), _F32)
            for e in range(N_EXPERTS):
                pt = jnp.where(c_i == pos_ref[e, 0] + shift[e], 1.0, pt)
            ptb = pt.astype(_BF16)
            for j2 in range(d // (2 * LANES)):
                r = _dot(ptb, h2_ref[:, j2 * 2 * LANES:(j2 + 1) * 2 * LANES])
                for h in range(2):
                    rows = pl.ds(k * DISPATCH_CHUNK * SUBLANES + 2 * j2 + h, DISPATCH_CHUNK, stride=SUBLANES)
                    xbuf[slot, rows, :] = r[:, h * LANES:(h + 1) * LANES]

    @pl.when(i > 0)
    def _():
        _dispatch_copies(start_ref, cnt_ref, xbuf, x_hbm, sems, i - 1, 1 - slot, lambda cp: cp.wait())

    _dispatch_copies(start_ref, cnt_ref, xbuf, x_hbm, sems, i, slot, lambda cp: cp.start())

    @pl.when(i == last)
    def _():
        _dispatch_copies(start_ref, cnt_ref, xbuf, x_hbm, sems, i, slot, lambda cp: cp.wait())


def _dispatch(h2, pos, start_tab, cnt_tab, cap):
    n_tok, d = h2.shape
    n_e, n_rows, _ = pos.shape
    assert d == SUBLANES * LANES, "one activation row must fill one 8 x 128 tile"
    n_tiles = n_tok // DISPATCH_TILE
    return pl.pallas_call(
        _dispatch_kernel,
        out_shape=jax.ShapeDtypeStruct((n_e, cap * SUBLANES, LANES), _F32),
        grid_spec=pltpu.PrefetchScalarGridSpec(
            num_scalar_prefetch=2,
            grid=(n_tiles,),
            in_specs=[
                pl.BlockSpec((n_e, 1, 1, LANES), lambda i, s, c: (0, i, 0, 0)),
                pl.BlockSpec((DISPATCH_TILE, d), lambda i, s, c: (i, 0)),
            ],
            out_specs=pl.BlockSpec(memory_space=pl.ANY),
            scratch_shapes=[
                pltpu.VMEM((2, _DISPATCH_ROWS * SUBLANES, LANES), _F32),
                pltpu.SemaphoreType.DMA((2,)),
            ],
        ),
        compiler_params=pltpu.CompilerParams(
            dimension_semantics=("arbitrary",), vmem_limit_bytes=VMEM_LIMIT),
        name="dispatch",
    )(start_tab, cnt_tab, pos.reshape(n_e, n_rows, 1, LANES), h2)


def _ffn_kernel(x_ref, wg_ref, wu_ref, wd_ref, y_ref, xb):
    f = pl.program_id(2)
    n_j = SUBLANES
    mt = xb.shape[0]

    @pl.when(f == 0)
    def _():
        for j in range(n_j):
            xb[:, j * LANES:(j + 1) * LANES] = x_ref[0, pl.ds(j, mt, stride=SUBLANES), :].astype(_BF16)

    def body(first):
        wg = wg_ref[0, 0].astype(_BF16)
        wu = wu_ref[0, 0].astype(_BF16)
        wd = wd_ref[0, 0].astype(_BF16)
        rb = min(FFN_SUB_ROWS, mt)
        for s in range(mt // rb):
            x = xb[s * rb:(s + 1) * rb, :]
            g = _dot(x, wg)
            u = _dot(x, wu)
            hid = (g * _sigmoid(g) * u).astype(_BF16)
            res = _dot(hid, wd)
            for j in range(n_j):
                part = res[:, j * LANES:(j + 1) * LANES]
                if first:
                    y_ref[0, j, s * rb:(s + 1) * rb, :] = part
                else:
                    y_ref[0, j, s * rb:(s + 1) * rb, :] += part

    pl.when(f == 0)(functools.partial(body, True))
    pl.when(f > 0)(functools.partial(body, False))


def _ffn(x_exp, w_gate, w_up, w_down, layer):
    n_e = x_exp.shape[0]
    n_j = SUBLANES
    cap = x_exp.shape[1] // n_j
    d = n_j * LANES
    d_exp = w_gate.shape[-1]
    mt = min(FFN_ROWS, cap)
    fc = min(FFN_COLS, d_exp)
    return pl.pallas_call(
        _ffn_kernel,
        out_shape=jax.ShapeDtypeStruct((n_e, n_j, cap, LANES), _F32),
        grid=(n_e, cap // mt, d_exp // fc),
        in_specs=[
            pl.BlockSpec((1, mt * n_j, LANES), lambda e, m, f: (e, m, 0)),
            pl.BlockSpec((1, 1, d, fc), lambda e, m, f: (layer, e, 0, f)),
            pl.BlockSpec((1, 1, d, fc), lambda e, m, f: (layer, e, 0, f)),
            pl.BlockSpec((1, 1, fc, d), lambda e, m, f: (layer, e, f, 0)),
        ],
        out_specs=pl.BlockSpec((1, n_j, mt, LANES), lambda e, m, f: (e, 0, m, 0)),
        scratch_shapes=[pltpu.VMEM((mt, d), _BF16)],
        compiler_params=pltpu.CompilerParams(
            dimension_semantics=("arbitrary", "arbitrary", "arbitrary"), vmem_limit_bytes=VMEM_LIMIT),
        name="ffn",
    )(x_exp, w_gate, w_up, w_down)


_COMBINE_ROWS = _round_up(N_EXPERTS * (COMBINE_TILE + 2 * SUBLANES), COMBINE_CHUNK)
_COL_BITS = 7
_COL_SPLIT = 1 << _COL_BITS
_NO_COLUMN = _COL_SPLIT * _COL_SPLIT - 1
assert _COMBINE_ROWS <= _NO_COLUMN


def _combine_windows(start_ref, cnt_ref, tile):
    out = []
    run = 0
    for e in range(N_EXPERTS):
        cnt = cnt_ref[tile * N_EXPERTS + e]
        st = start_ref[tile * N_EXPERTS + e]
        a0 = (st >> 3) << 3
        a1 = ((st + cnt + SUBLANES - 1) >> 3) << 3
        rows = jnp.where(cnt > 0, a1 - a0, 0)
        out.append((a0, rows, run))
        run = run + rows
    return out, run


def _combine_copies(start_ref, cnt_ref, y_hbm, ybuf, sems, tile, slot, act):
    wins, _ = _combine_windows(start_ref, cnt_ref, tile)
    for e, (a0, rows, boff) in enumerate(wins):
        @pl.when(rows > 0)
        def _(a0=a0, rows=rows, boff=boff, e=e):
            n = pl.multiple_of(rows, SUBLANES)
            src = y_hbm.at[e, :, pl.ds(pl.multiple_of(a0, SUBLANES), n), :]
            dst = ybuf.at[slot, :, pl.ds(pl.multiple_of(boff, SUBLANES), n), :]
            act(pltpu.make_async_copy(src, dst, sems.at[slot]))


def _combine_kernel(start_ref, cnt_ref, pos_ref, aff_ref, x1_ref, gt_ref, gf_ref, y_hbm, o_ref,
                    ybuf, acc, sems, *, final_norm):
    i = pl.program_id(0)
    n_tiles = pl.num_programs(0)
    slot = i % 2
    tt, d = x1_ref.shape

    @pl.when(i == 0)
    def _():
        ybuf[...] = jnp.zeros_like(ybuf)
        _combine_copies(start_ref, cnt_ref, y_hbm, ybuf, sems, i, slot, lambda cp: cp.start())

    @pl.when(i + 1 < n_tiles)
    def _():
        _combine_copies(start_ref, cnt_ref, y_hbm, ybuf, sems, i + 1, 1 - slot, lambda cp: cp.start())

    _combine_copies(start_ref, cnt_ref, y_hbm, ybuf, sems, i, slot, lambda cp: cp.wait())

    wins, n_rows = _combine_windows(start_ref, cnt_ref, i)
    n_e = pos_ref.shape[1]
    ch = COMBINE_CHUNK

    lane_e = lax.broadcasted_iota(jnp.int32, (1, n_e), 1)
    shift = jnp.zeros((1, n_e), jnp.int32)
    for e, (a0, _, boff) in enumerate(wins):
        shift = jnp.where(lane_e == e, boff - a0, shift)
    pos = pos_ref[...]
    col = jnp.where(pos >= 0, pos + shift, _NO_COLUMN)
    col_hi = (col >> _COL_BITS).astype(_F32).astype(_BF16)
    col_lo = (col & (_COL_SPLIT - 1)).astype(_F32).astype(_BF16)
    aff = aff_ref[...]
    a_hi = aff.astype(_BF16)
    a_lo = (aff - a_hi.astype(_F32)).astype(_BF16)

    acc[...] = jnp.zeros_like(acc)
    for k in range(_COMBINE_ROWS // ch):
        @pl.when(k * ch < n_rows)
        def _(k=k):
            c_i = k * ch + lax.broadcasted_iota(jnp.int32, (1, ch), 1)
            owner = jnp.zeros((1, ch), jnp.int32)
            for _, rows, boff in wins:
                owner = owner + jnp.where(c_i >= boff + rows, 1, 0)
            spread = jnp.where(lax.broadcasted_iota(jnp.int32, (n_e, ch), 0) == owner, 1.0, 0.0).astype(_BF16)
            col_c = _COL_SPLIT * _dot(col_hi, spread) + _dot(col_lo, spread)
            hit = col_c == c_i.astype(_F32)
            ph = jnp.where(hit, _dot(a_hi, spread), 0.0).astype(_BF16)
            plo = jnp.where(hit, _dot(a_lo, spread), 0.0).astype(_BF16)
            for j2 in range(d // (2 * LANES)):
                y = jnp.concatenate([ybuf[slot, 2 * j2, pl.ds(k * ch, ch), :],
                                     ybuf[slot, 2 * j2 + 1, pl.ds(k * ch, ch), :]], axis=1)
                yh = y.astype(_BF16)
                ylo = (y - yh.astype(_F32)).astype(_BF16)
                acc[:, j2 * 2 * LANES:(j2 + 1) * 2 * LANES] += _dot(ph, yh) + (_dot(ph, ylo) + _dot(plo, yh))

    x2 = x1_ref[...] + gt_ref[0] * acc[...]
    if final_norm:
        x2 = x2 * lax.rsqrt(jnp.mean(x2 * x2, axis=-1, keepdims=True) + EPS) * gf_ref[...]
    o_ref[...] = x2


def _combine(y_exp, pos_t, aff_tok, x1, gt2, g_final, start_tab, cnt_tab, seq_len, final_norm):
    n_tok, d = x1.shape
    n_e = y_exp.shape[0]
    n_tiles = n_tok // COMBINE_TILE
    tiles_per_seq = seq_len // COMBINE_TILE
    return pl.pallas_call(
        functools.partial(_combine_kernel, final_norm=final_norm),
        out_shape=jax.ShapeDtypeStruct((n_tok, d), _F32),
        grid_spec=pltpu.PrefetchScalarGridSpec(
            num_scalar_prefetch=2,
            grid=(n_tiles,),
            in_specs=[
                pl.BlockSpec((COMBINE_TILE, n_e), lambda i, s, c: (i, 0)),
                pl.BlockSpec((COMBINE_TILE, n_e), lambda i, s, c: (i, 0)),
                pl.BlockSpec((COMBINE_TILE, d), lambda i, s, c: (i, 0)),
                pl.BlockSpec((1, 1, d), lambda i, s, c: (i // tiles_per_seq, 0, 0)),
                pl.BlockSpec((1, d), lambda i, s, c: (0, 0)),
                pl.BlockSpec(memory_space=pl.ANY),
            ],
            out_specs=pl.BlockSpec((COMBINE_TILE, d), lambda i, s, c: (i, 0)),
            scratch_shapes=[
                pltpu.VMEM((2, d // LANES, _COMBINE_ROWS, LANES), _F32),
                pltpu.VMEM((COMBINE_TILE, d), _F32),
                pltpu.SemaphoreType.DMA((2,)),
            ],
        ),
        compiler_params=pltpu.CompilerParams(
            dimension_semantics=("arbitrary",), vmem_limit_bytes=VMEM_LIMIT),
        name="combine",
    )(start_tab, cnt_tab, pos_t, aff_tok, x1, gt2, g_final, y_exp)


def _moe(x1, h2, aff_tok, gt2, g_final, w_gate, w_up, w_down, layer, final_norm):
    bsz, s, d = x1.shape
    n_tok, n_e = aff_tok.shape
    cap = max(1, CAPACITY_FACTOR * n_tok // n_e)
    pos, start_rep = _route(aff_tok.T, cap)
    row_start = start_rep[:, :, 0]

    def tile_tables(tile):
        first = row_start[:, ::tile // LANES]
        end = jnp.concatenate([first[:, 1:], jnp.full((n_e, 1), cap, jnp.int32)], axis=1)
        return first.T.reshape(-1), (end - first).T.reshape(-1)

    x_exp = _dispatch(h2.reshape(n_tok, d), pos, *tile_tables(DISPATCH_TILE), cap)
    y_exp = _ffn(x_exp, w_gate, w_up, w_down, layer)
    x2 = _combine(y_exp, pos.reshape(n_e, n_tok).T, aff_tok, x1.reshape(n_tok, d), gt2, g_final,
                  *tile_tables(COMBINE_TILE), s, final_norm)
    return x2.reshape(bsz, s, d)


def kernel(x_prompt, x_sample, c_prompt, c_sample, w_ada, b_ada, g_norm1, g_norm2, w_in, b_in, conv_w, conv_b, conv_ln_g, conv_ln_b, sgu_ln_g, sgu_ln_b, sgu_w, sgu_b, pool_w, pool_scale, w_branch, w_out, w_router, w_e_gate, w_e_up, w_e_down, g_final):
    depth = w_ada.shape[0]
    d = x_prompt.shape[-1]
    groups = ((x_prompt, c_prompt), (x_sample, c_sample))

    n_c = sum(c.shape[0] for _, c in groups)
    pad = -n_c % SUBLANES
    c_all = jnp.concatenate([c for _, c in groups] + [jnp.zeros((pad, d), _F32)], axis=0)
    mod_all = _adaln(c_all, w_ada, b_ada)

    row = lambda v: v.reshape(1, -1)
    layers = []
    for l in range(depth):
        wr_hi = w_router[l].astype(_BF16)
        layers.append(dict(
            g1=row(g_norm1[l]), g2=row(g_norm2[l]),
            w_in=w_in[l].astype(_BF16), b_in=row(b_in[l]),
            conv_w=jnp.broadcast_to(conv_w[l][:, None, :], (CONV_WIDTH, SUBLANES, d)), conv_b=row(conv_b[l]),
            conv_ln_g=row(conv_ln_g[l]), conv_ln_b=row(conv_ln_b[l]),
            sgu_ln_g=row(sgu_ln_g[l]), sgu_ln_b=row(sgu_ln_b[l]),
            sgu_w=sgu_w[l].astype(_BF16),
            sgu_b=jnp.repeat(sgu_b[l].T, d // SGU_HEADS, axis=1),
            pool_w=pool_w[l].astype(_BF16), pool_scale=row(pool_scale[l]),
            w_branch=w_branch[l].astype(_BF16), w_out=w_out[l].astype(_BF16),
            w_router_hi=wr_hi, w_router_lo=(w_router[l] - wr_hi.astype(_F32)).astype(_BF16),
        ))
    gf = row(g_final)

    outs = []
    c0 = 0
    for x, c in groups:
        bsz = x.shape[0]
        for l in range(depth):
            mod = mod_all[l, c0:c0 + bsz].reshape(bsz, 6, d)
            x1, h2, aff_tok = _mixer(x, mod, layers[l])
            x = _moe(x1, h2, aff_tok, mod[:, 5:6, :], gf, w_e_gate, w_e_up, w_e_down, l,
                     final_norm=(l == depth - 1))
        outs.append(x)
        c0 += bsz
    return tuple(outs)
```

```python
import functools

import jax
import jax.numpy as jnp
from jax import lax
from jax.experimental import pallas as pl
from jax.experimental.pallas import tpu as pltpu

EPS = 1e-6
CONV_WIDTH = 31
CONV_PAD = CONV_WIDTH // 2
SGU_CHUNK = 128
SGU_HEADS = 8
POOL_WINDOWS = (2, 4, 8, 16)
N_BRANCH = 3
N_EXPERTS = 16
CAPACITY_FACTOR = 2

LANES = 128
SUBLANES = 8
HALO = 16
MIX_TILE = 256
DISPATCH_TILE = 128
COMBINE_TILE = 256
DISPATCH_CHUNK = 384
COMBINE_CHUNK = 768
FFN_ROWS = 1024
FFN_COLS = 1024
FFN_SUB_ROWS = 512
VMEM_LIMIT = 56 * 1024 * 1024
NOT_SELECTED = -(1 << 20)

_F32 = jnp.float32
_BF16 = jnp.bfloat16


def _dot(a, b):
    return jnp.dot(a, b, preferred_element_type=_F32)


def _sigmoid(x):
    return 1.0 / (1.0 + jnp.exp(-x))


def _rms_mod(x, scale, shift):
    return x * lax.rsqrt(jnp.mean(x * x, axis=-1, keepdims=True) + EPS) * scale + shift


def _layernorm(x, g, b):
    mu = jnp.mean(x, axis=-1, keepdims=True)
    xc = x - mu
    var = jnp.mean(xc * xc, axis=-1, keepdims=True)
    return xc * lax.rsqrt(var + EPS) * g + b


def _adaln_kernel(c_ref, w_ref, b_ref, o_ref):
    c = c_ref[...]
    s = c * _sigmoid(c)
    o_ref[0] = jnp.dot(s, w_ref[0], preferred_element_type=_F32,
                       precision=lax.Precision.HIGHEST) + b_ref[0]


def _adaln(c_all, w_ada, b_ada):
    depth, d, n = w_ada.shape
    rows = c_all.shape[0]
    return pl.pallas_call(
        _adaln_kernel,
        out_shape=jax.ShapeDtypeStruct((depth, rows, n), _F32),
        grid=(depth, n // d),
        in_specs=[
            pl.BlockSpec((rows, d), lambda l, j: (0, 0)),
            pl.BlockSpec((1, d, d), lambda l, j: (l, 0, j)),
            pl.BlockSpec((1, 1, d), lambda l, j: (l, 0, j)),
        ],
        out_specs=pl.BlockSpec((1, rows, d), lambda l, j: (l, 0, j)),
        name="adaln",
    )(c_all, w_ada, b_ada.reshape(depth, 1, n))


def _window_sum(x, w):
    n = x.shape[0]
    a = x
    span = 1
    while span < w // 2:
        a = a + pltpu.roll(a, n - span, axis=0)
        span *= 2
    return pltpu.roll(a, w // 2, axis=0) + a


def _mixer_kernel(xp_ref, x_ref, xn_ref, mod_ref, g1_ref, g2_ref, win_ref, bin_ref,
                  convw_ref, convb_ref, clng_ref, clnb_ref, slng_ref, slnb_ref, sguw_ref, sgub_ref,
                  poolw_ref, pscale_ref, wbr_ref, wout_ref, wrh_ref, wrl_ref,
                  x1_ref, h2_ref, aff_ref,
                  hbuf, gbuf, sbuf, cbuf, ybuf, *, seq_len):
    tm = x_ref.shape[1]
    d = x_ref.shape[2]
    i = pl.program_id(1)
    n_tiles = pl.num_programs(1)

    sh1 = mod_ref[0, 0:1, :]
    sc1 = mod_ref[0, 1:2, :]
    gt1 = mod_ref[0, 2:3, :]
    sh2 = mod_ref[0, 3:4, :]
    sc2 = mod_ref[0, 4:5, :]
    scale1 = g1_ref[...] * (1.0 + sc1)
    scale2 = g2_ref[...] * (1.0 + sc2)

    xm = x_ref[0]
    hbuf[0:HALO, :] = _rms_mod(xp_ref[0], scale1, sh1).astype(_BF16)
    hbuf[HALO:HALO + tm, :] = _rms_mod(xm, scale1, sh1).astype(_BF16)
    hbuf[HALO + tm:, :] = _rms_mod(xn_ref[0], scale1, sh1).astype(_BF16)

    rows = lax.broadcasted_iota(jnp.int32, (tm + 2 * HALO, 1), 0)
    valid = jnp.logical_and(jnp.logical_or(rows >= HALO, i > 0),
                            jnp.logical_or(rows < tm + HALO, i < n_tiles - 1))

    o_u = 2 * d
    o_v = 3 * d
    o_c = 4 * d
    o_g = 5 * d

    nt_cols = 256
    for n0 in range(0, d, nt_cols):
        a = _dot(hbuf[...], win_ref[:, n0:n0 + nt_cols]) + bin_ref[:, n0:n0 + nt_cols]
        gt = _dot(hbuf[...], win_ref[:, d + n0:d + n0 + nt_cols]) + bin_ref[:, d + n0:d + n0 + nt_cols]
        gbuf[:, n0:n0 + nt_cols] = jnp.where(valid, a * _sigmoid(gt), 0.0)
    n_sh = sbuf.shape[1]
    for r in range(1, SUBLANES):
        sbuf[r - 1] = gbuf[r:r + n_sh, :]
    n_grp = 4
    rc = n_grp * SUBLANES
    for c in range(tm // rc):
        r0 = c * rc
        accs = [None] * n_grp
        for k in range(CONV_WIDTH):
            q, r = divmod(HALO - CONV_PAD + k, SUBLANES)
            w8 = convw_ref[k]
            for g in range(n_grp):
                lo = r0 + (q + g) * SUBLANES
                src = gbuf[lo:lo + SUBLANES, :] if r == 0 else sbuf[r - 1, lo:lo + SUBLANES, :]
                term = src * w8
                accs[g] = term if accs[g] is None else accs[g] + term
        acc = jnp.concatenate(accs, axis=0)
        y = _layernorm(acc + convb_ref[...], clng_ref[...], clnb_ref[...])
        ybuf[0, r0:r0 + rc, :] = (y * _sigmoid(y)).astype(_BF16)

    hm = hbuf[HALO:HALO + tm, :]
    pu = _dot(hm, win_ref[:, o_u:o_v]) + bin_ref[:, o_u:o_v]
    pv = _dot(hm, win_ref[:, o_v:o_c]) + bin_ref[:, o_v:o_c]
    vn = _layernorm(pv, slng_ref[...], slnb_ref[...]).astype(_BF16)
    dh = d // SGU_HEADS
    for n in range(tm // SGU_CHUNK):
        r0 = n * SGU_CHUNK
        for hd in range(SGU_HEADS):
            c0 = hd * dh
            vm = _dot(sguw_ref[hd], vn[r0:r0 + SGU_CHUNK, c0:c0 + dh]) + sgub_ref[:, c0:c0 + dh]
            ybuf[1, r0:r0 + SGU_CHUNK, c0:c0 + dh] = (pu[r0:r0 + SGU_CHUNK, c0:c0 + dh] * vm).astype(_BF16)

    pc = _dot(hbuf[...], win_ref[:, o_c:o_g]) + bin_ref[:, o_c:o_g]
    cbuf[...] = jnp.where(valid, pc, 0.0)
    pos = i * tm + lax.broadcasted_iota(jnp.int32, (tm, 1), 0)
    gc = d // len(POOL_WINDOWS)
    for g, w in enumerate(POOL_WINDOWS):
        c0 = g * gc
        half = w // 2
        tot = _window_sum(cbuf[:, c0:c0 + gc], w)[HALO:HALO + tm]
        lo = jnp.maximum(pos - half, 0)
        hi = jnp.minimum(pos - half + w, seq_len)
        mean = tot / (hi - lo).astype(_F32)
        diff = mean - cbuf[HALO:HALO + tm, c0:c0 + gc]
        yc = _dot(diff.astype(_BF16), poolw_ref[g]) * pscale_ref[:, c0:c0 + gc]
        ybuf[2, :, c0:c0 + gc] = yc.astype(_BF16)

    parts = []
    for n0 in range(0, d, nt_cols):
        part = None
        for b in range(N_BRANCH):
            lo = o_g + b * d + n0
            gate = _sigmoid(_dot(hm, win_ref[:, lo:lo + nt_cols]) + bin_ref[:, lo:lo + nt_cols])
            term = _dot(ybuf[b], wbr_ref[b, :, n0:n0 + nt_cols]) * gate
            part = term if part is None else part + term
        parts.append(part.astype(_BF16))
    mix = _dot(jnp.concatenate(parts, axis=1), wout_ref[...])
    x1 = xm + gt1 * mix
    x1_ref[0] = x1

    h2 = _rms_mod(x1, scale2, sh2)
    h2_hi = h2.astype(_BF16)
    h2_ref[0] = h2_hi
    h2_lo = (h2 - h2_hi.astype(_F32)).astype(_BF16)
    logits = _dot(h2_hi, wrh_ref[...]) + (_dot(h2_lo, wrh_ref[...]) + _dot(h2_hi, wrl_ref[...]))
    m = jnp.max(logits, axis=1, keepdims=True)
    ex = jnp.exp(logits - m)
    aff_ref[...] = ex / jnp.sum(ex, axis=1, keepdims=True)


def _const_spec(shape):
    nd = len(shape)
    return pl.BlockSpec(shape, lambda b, i: (0,) * nd, pipeline_mode=pl.Buffered(1))


def _mixer(x, mod, lw, tm=MIX_TILE):
    bsz, s, d = x.shape
    nt = s // tm
    hb = tm // HALO
    n_in = lw["w_in"].shape[1]
    x_halo = lambda f: pl.BlockSpec((1, HALO, d), f)
    return pl.pallas_call(
        functools.partial(_mixer_kernel, seq_len=s),
        out_shape=(
            jax.ShapeDtypeStruct((bsz, s, d), _F32),
            jax.ShapeDtypeStruct((bsz, s, d), _BF16),
            jax.ShapeDtypeStruct((bsz * s, N_EXPERTS), _F32),
        ),
        grid=(bsz, nt),
        in_specs=[
            x_halo(lambda b, i: (b, jnp.maximum(i * hb - 1, 0), 0)),
            pl.BlockSpec((1, tm, d), lambda b, i: (b, i, 0)),
            x_halo(lambda b, i: (b, jnp.minimum((i + 1) * hb, s // HALO - 1), 0)),
            pl.BlockSpec((1, 6, d), lambda b, i: (b, 0, 0)),
            _const_spec((1, d)), _const_spec((1, d)),
            _const_spec((d, n_in)), _const_spec((1, n_in)),
            _const_spec((CONV_WIDTH, SUBLANES, d)), _const_spec((1, d)), _const_spec((1, d)), _const_spec((1, d)),
            _const_spec((1, d)), _const_spec((1, d)),
            _const_spec((SGU_HEADS, SGU_CHUNK, SGU_CHUNK)), _const_spec((SGU_CHUNK, d)),
            _const_spec(lw["pool_w"].shape), _const_spec((1, d)),
            _const_spec((N_BRANCH, d, d)), _const_spec((d, d)),
            _const_spec((d, N_EXPERTS)), _const_spec((d, N_EXPERTS)),
        ],
        out_specs=(
            pl.BlockSpec((1, tm, d), lambda b, i: (b, i, 0)),
            pl.BlockSpec((1, tm, d), lambda b, i: (b, i, 0)),
            pl.BlockSpec((tm, N_EXPERTS), lambda b, i: (b * nt + i, 0)),
        ),
        scratch_shapes=[
            pltpu.VMEM((tm + 2 * HALO, d), _BF16),
            pltpu.VMEM((tm + 2 * HALO, d), _F32),
            pltpu.VMEM((SUBLANES - 1, tm + 2 * HALO - SUBLANES, d), _F32),
            pltpu.VMEM((tm + 2 * HALO, d), _F32),
            pltpu.VMEM((N_BRANCH, tm, d), _BF16),
        ],
        compiler_params=pltpu.CompilerParams(
            dimension_semantics=("arbitrary", "arbitrary"), vmem_limit_bytes=VMEM_LIMIT),
        name="mixer",
    )(x, x, x, mod, lw["g1"], lw["g2"], lw["w_in"], lw["b_in"], lw["conv_w"], lw["conv_b"],
      lw["conv_ln_g"], lw["conv_ln_b"], lw["sgu_ln_g"], lw["sgu_ln_b"], lw["sgu_w"], lw["sgu_b"],
      lw["pool_w"], lw["pool_scale"], lw["w_branch"], lw["w_out"], lw["w_router_hi"], lw["w_router_lo"])


def _route_kernel(aff_ref, pos_ref, start_ref, *, cap, n_tok):
    n_e, n_rows, _ = aff_ref.shape
    keys = pltpu.bitcast(aff_ref[...], jnp.int32)

    def count(ind):
        return jnp.sum(jnp.sum(ind, axis=1, keepdims=True), axis=2, keepdims=True)

    def thr_step(it, prefix):
        cand = prefix | jnp.left_shift(jnp.int32(1), 30 - it)
        c = count(jnp.where(keys >= cand, 1.0, 0.0))
        return jnp.where(c >= cap, cand, prefix)

    thr = lax.fori_loop(0, 31, thr_step, jnp.zeros((n_e, 1, 1), jnp.int32))
    gt = keys > thr
    eq = keys == thr
    need = cap - count(jnp.where(gt, 1.0, 0.0))
    tidx = (lax.broadcasted_iota(jnp.int32, (1, n_rows, LANES), 1) * LANES
            + lax.broadcasted_iota(jnp.int32, (1, n_rows, LANES), 2))
    n_bits = max(1, (n_tok - 1).bit_length())

    def tie_step(it, bound):
        cand = bound + jnp.left_shift(jnp.int32(1), n_bits - 1 - it)
        c = count(jnp.where(eq, jnp.where(tidx < cand, 1.0, 0.0), 0.0))
        return jnp.where(c < need, cand, bound)

    bound = lax.fori_loop(0, n_bits, tie_step, jnp.zeros((n_e, 1, 1), jnp.int32))
    sel = jnp.where(gt, 1.0, jnp.where(eq, jnp.where(tidx <= bound, 1.0, 0.0), 0.0))

    r_i = lax.broadcasted_iota(jnp.int32, (LANES, LANES), 0)
    c_i = lax.broadcasted_iota(jnp.int32, (LANES, LANES), 1)
    upper = jnp.where(r_i <= c_i, 1.0, 0.0).astype(_BF16)
    ones = jnp.ones((LANES, LANES), _BF16)
    rr = lax.broadcasted_iota(jnp.int32, (n_rows, n_rows), 0)
    rc = lax.broadcasted_iota(jnp.int32, (n_rows, n_rows), 1)
    lower = jnp.where(rc < rr, 1.0, 0.0).astype(_BF16)
    for e in range(n_e):
        s = sel[e]
        sb = s.astype(_BF16)
        incl = _dot(sb, upper)
        tot = _dot(sb, ones)
        start = _dot(lower, tot.astype(_BF16))
        slot = (start + incl).astype(jnp.int32) - 1
        pos_ref[e] = jnp.where(s > 0.5, slot, NOT_SELECTED)
        start_ref[e] = start.astype(jnp.int32)


def _route(aff_t, cap):
    n_e, n_tok = aff_t.shape
    n_rows = n_tok // LANES
    return pl.pallas_call(
        functools.partial(_route_kernel, cap=cap, n_tok=n_tok),
        out_shape=(
            jax.ShapeDtypeStruct((n_e, n_rows, LANES), jnp.int32),
            jax.ShapeDtypeStruct((n_e, n_rows, LANES), jnp.int32),
        ),
        name="route",
    )(aff_t.reshape(n_e, n_rows, LANES))


def _round_up(n, m):
    return -(-n // m) * m


_DISPATCH_ROWS = _round_up(N_EXPERTS * DISPATCH_TILE, DISPATCH_CHUNK)


def _dispatch_copies(start_ref, cnt_ref, xbuf, x_hbm, sems, tile, slot, act):
    run = 0
    for e in range(N_EXPERTS):
        cnt = cnt_ref[tile * N_EXPERTS + e]
        st = start_ref[tile * N_EXPERTS + e]

        @pl.when(cnt > 0)
        def _(run=run, cnt=cnt, st=st, e=e):
            src = xbuf.at[slot, pl.ds(pl.multiple_of(run * SUBLANES, SUBLANES), cnt * SUBLANES)]
            dst = x_hbm.at[e, pl.ds(pl.multiple_of(st * SUBLANES, SUBLANES), cnt * SUBLANES)]
            act(pltpu.make_async_copy(src, dst, sems.at[slot]))
        run = run + cnt


def _dispatch_kernel(start_ref, cnt_ref, pos_ref, h2_ref, x_hbm, xbuf, sems):
    i = pl.program_id(0)
    last = pl.num_programs(0) - 1
    slot = i % 2
    tt, d = h2_ref.shape

    shift = []
    run = 0
    for e in range(N_EXPERTS):
        shift.append(run - start_ref[i * N_EXPERTS + e])
        run = run + cnt_ref[i * N_EXPERTS + e]
    n_rows = run

    for k in range(_DISPATCH_ROWS // DISPATCH_CHUNK):
        @pl.when(k * DISPATCH_CHUNK < n_rows)
        def _(k=k):
            c_i = k * DISPATCH_CHUNK + lax.broadcasted_iota(jnp.int32, (DISPATCH_CHUNK, tt), 0)
            pt = jnp.zeros((DISPATCH_CHUNK, tt), _F32)
            for e in range(N_EXPERTS):
                pt = jnp.where(c_i == pos_ref[e, 0] + shift[e], 1.0, pt)
            ptb = pt.astype(_BF16)
            for j2 in range(d // (2 * LANES)):
                r = _dot(ptb, h2_ref[:, j2 * 2 * LANES:(j2 + 1) * 2 * LANES])
                for h in range(2):
                    rows = pl.ds(k * DISPATCH_CHUNK * SUBLANES + 2 * j2 + h, DISPATCH_CHUNK, stride=SUBLANES)
                    xbuf[slot, rows, :] = r[:, h * LANES:(h + 1) * LANES]

    @pl.when(i > 0)
    def _():
        _dispatch_copies(start_ref, cnt_ref, xbuf, x_hbm, sems, i - 1, 1 - slot, lambda cp: cp.wait())

    _dispatch_copies(start_ref, cnt_ref, xbuf, x_hbm, sems, i, slot, lambda cp: cp.start())

    @pl.when(i == last)
    def _():
        _dispatch_copies(start_ref, cnt_ref, xbuf, x_hbm, sems, i, slot, lambda cp: cp.wait())


def _dispatch(h2, pos, start_tab, cnt_tab, cap):
    n_tok, d = h2.shape
    n_e, n_rows, _ = pos.shape
    assert d == SUBLANES * LANES, "one activation row must fill one 8 x 128 tile"
    n_tiles = n_tok // DISPATCH_TILE
    return pl.pallas_call(
        _dispatch_kernel,
        out_shape=jax.ShapeDtypeStruct((n_e, cap * SUBLANES, LANES), _F32),
        grid_spec=pltpu.PrefetchScalarGridSpec(
            num_scalar_prefetch=2,
            grid=(n_tiles,),
            in_specs=[
                pl.BlockSpec((n_e, 1, 1, LANES), lambda i, s, c: (0, i, 0, 0)),
                pl.BlockSpec((DISPATCH_TILE, d), lambda i, s, c: (i, 0)),
            ],
            out_specs=pl.BlockSpec(memory_space=pl.ANY),
            scratch_shapes=[
                pltpu.VMEM((2, _DISPATCH_ROWS * SUBLANES, LANES), _F32),
                pltpu.SemaphoreType.DMA((2,)),
            ],
        ),
        compiler_params=pltpu.CompilerParams(
            dimension_semantics=("arbitrary",), vmem_limit_bytes=VMEM_LIMIT),
        name="dispatch",
    )(start_tab, cnt_tab, pos.reshape(n_e, n_rows, 1, LANES), h2)


def _ffn_kernel(x_ref, wg_ref, wu_ref, wd_ref, y_ref, xb):
    f = pl.program_id(2)
    n_j = SUBLANES
    mt = xb.shape[0]

    @pl.when(f == 0)
    def _():
        for j in range(n_j):
            xb[:, j * LANES:(j + 1) * LANES] = x_ref[0, pl.ds(j, mt, stride=SUBLANES), :].astype(_BF16)

    def body(first):
        wg = wg_ref[0, 0].astype(_BF16)
        wu = wu_ref[0, 0].astype(_BF16)
        wd = wd_ref[0, 0].astype(_BF16)
        rb = min(FFN_SUB_ROWS, mt)
        for s in range(mt // rb):
            x = xb[s * rb:(s + 1) * rb, :]
            g = _dot(x, wg)
            u = _dot(x, wu)
            hid = (g * _sigmoid(g) * u).astype(_BF16)
            res = _dot(hid, wd)
            for j in range(n_j):
                part = res[:, j * LANES:(j + 1) * LANES]
                if first:
                    y_ref[0, j, s * rb:(s + 1) * rb, :] = part
                else:
                    y_ref[0, j, s * rb:(s + 1) * rb, :] += part

    pl.when(f == 0)(functools.partial(body, True))
    pl.when(f > 0)(functools.partial(body, False))


def _ffn(x_exp, w_gate, w_up, w_down, layer):
    n_e = x_exp.shape[0]
    n_j = SUBLANES
    cap = x_exp.shape[1] // n_j
    d = n_j * LANES
    d_exp = w_gate.shape[-1]
    mt = min(FFN_ROWS, cap)
    fc = min(FFN_COLS, d_exp)
    return pl.pallas_call(
        _ffn_kernel,
        out_shape=jax.ShapeDtypeStruct((n_e, n_j, cap, LANES), _F32),
        grid=(n_e, cap // mt, d_exp // fc),
        in_specs=[
            pl.BlockSpec((1, mt * n_j, LANES), lambda e, m, f: (e, m, 0)),
            pl.BlockSpec((1, 1, d, fc), lambda e, m, f: (layer, e, 0, f)),
            pl.BlockSpec((1, 1, d, fc), lambda e, m, f: (layer, e, 0, f)),
            pl.BlockSpec((1, 1, fc, d), lambda e, m, f: (layer, e, f, 0)),
        ],
        out_specs=pl.BlockSpec((1, n_j, mt, LANES), lambda e, m, f: (e, 0, m, 0)),
        scratch_shapes=[pltpu.VMEM((mt, d), _BF16)],
        compiler_params=pltpu.CompilerParams(
            dimension_semantics=("arbitrary", "arbitrary", "arbitrary"), vmem_limit_bytes=VMEM_LIMIT),
        name="ffn",
    )(x_exp, w_gate, w_up, w_down)


_COMBINE_ROWS = _round_up(N_EXPERTS * (COMBINE_TILE + 2 * SUBLANES), COMBINE_CHUNK)
_COL_BITS = 7
_COL_SPLIT = 1 << _COL_BITS
_NO_COLUMN = _COL_SPLIT * _COL_SPLIT - 1
assert _COMBINE_ROWS <= _NO_COLUMN


def _combine_windows(start_ref, cnt_ref, tile):
    out = []
    run = 0
    for e in range(N_EXPERTS):
        cnt = cnt_ref[tile * N_EXPERTS + e]
        st = start_ref[tile * N_EXPERTS + e]
        a0 = (st >> 3) << 3
        a1 = ((st + cnt + SUBLANES - 1) >> 3) << 3
        rows = jnp.where(cnt > 0, a1 - a0, 0)
        out.append((a0, rows, run))
        run = run + rows
    return out, run


def _combine_copies(start_ref, cnt_ref, y_hbm, ybuf, sems, tile, slot, act):
    wins, _ = _combine_windows(start_ref, cnt_ref, tile)
    for e, (a0, rows, boff) in enumerate(wins):
        @pl.when(rows > 0)
        def _(a0=a0, rows=rows, boff=boff, e=e):
            n = pl.multiple_of(rows, SUBLANES)
            src = y_hbm.at[e, :, pl.ds(pl.multiple_of(a0, SUBLANES), n), :]
            dst = ybuf.at[slot, :, pl.ds(pl.multiple_of(boff, SUBLANES), n), :]
            act(pltpu.make_async_copy(src, dst, sems.at[slot]))


def _combine_kernel(start_ref, cnt_ref, pos_ref, aff_ref, x1_ref, gt_ref, gf_ref, y_hbm, o_ref,
                    ybuf, acc, sems, *, final_norm):
    i = pl.program_id(0)
    n_tiles = pl.num_programs(0)
    slot = i % 2
    tt, d = x1_ref.shape

    @pl.when(i == 0)
    def _():
        ybuf[...] = jnp.zeros_like(ybuf)
        _combine_copies(start_ref, cnt_ref, y_hbm, ybuf, sems, i, slot, lambda cp: cp.start())

    @pl.when(i + 1 < n_tiles)
    def _():
        _combine_copies(start_ref, cnt_ref, y_hbm, ybuf, sems, i + 1, 1 - slot, lambda cp: cp.start())

    _combine_copies(start_ref, cnt_ref, y_hbm, ybuf, sems, i, slot, lambda cp: cp.wait())

    wins, n_rows = _combine_windows(start_ref, cnt_ref, i)
    n_e = pos_ref.shape[1]
    ch = COMBINE_CHUNK

    lane_e = lax.broadcasted_iota(jnp.int32, (1, n_e), 1)
    shift = jnp.zeros((1, n_e), jnp.int32)
    for e, (a0, _, boff) in enumerate(wins):
        shift = jnp.where(lane_e == e, boff - a0, shift)
    pos = pos_ref[...]
    col = jnp.where(pos >= 0, pos + shift, _NO_COLUMN)
    col_hi = (col >> _COL_BITS).astype(_F32).astype(_BF16)
    col_lo = (col & (_COL_SPLIT - 1)).astype(_F32).astype(_BF16)
    aff = aff_ref[...]
    a_hi = aff.astype(_BF16)
    a_lo = (aff - a_hi.astype(_F32)).astype(_BF16)

    acc[...] = jnp.zeros_like(acc)
    for k in range(_COMBINE_ROWS // ch):
        @pl.when(k * ch < n_rows)
        def _(k=k):
            c_i = k * ch + lax.broadcasted_iota(jnp.int32, (1, ch), 1)
            owner = jnp.zeros((1, ch), jnp.int32)
            for _, rows, boff in wins:
                owner = owner + jnp.where(c_i >= boff + rows, 1, 0)
            spread = jnp.where(lax.broadcasted_iota(jnp.int32, (n_e, ch), 0) == owner, 1.0, 0.0).astype(_BF16)
            col_c = _COL_SPLIT * _dot(col_hi, spread) + _dot(col_lo, spread)
            hit = col_c == c_i.astype(_F32)
            ph = jnp.where(hit, _dot(a_hi, spread), 0.0).astype(_BF16)
            plo = jnp.where(hit, _dot(a_lo, spread), 0.0).astype(_BF16)
            for j2 in range(d // (2 * LANES)):
                y = jnp.concatenate([ybuf[slot, 2 * j2, pl.ds(k * ch, ch), :],
                                     ybuf[slot, 2 * j2 + 1, pl.ds(k * ch, ch), :]], axis=1)
                yh = y.astype(_BF16)
                ylo = (y - yh.astype(_F32)).astype(_BF16)
                acc[:, j2 * 2 * LANES:(j2 + 1) * 2 * LANES] += _dot(ph, yh) + (_dot(ph, ylo) + _dot(plo, yh))

    x2 = x1_ref[...] + gt_ref[0] * acc[...]
    if final_norm:
        x2 = x2 * lax.rsqrt(jnp.mean(x2 * x2, axis=-1, keepdims=True) + EPS) * gf_ref[...]
    o_ref[...] = x2


def _combine(y_exp, pos_t, aff_tok, x1, gt2, g_final, start_tab, cnt_tab, seq_len, final_norm):
    n_tok, d = x1.shape
    n_e = y_exp.shape[0]
    n_tiles = n_tok // COMBINE_TILE
    tiles_per_seq = seq_len // COMBINE_TILE
    return pl.pallas_call(
        functools.partial(_combine_kernel, final_norm=final_norm),
        out_shape=jax.ShapeDtypeStruct((n_tok, d), _F32),
        grid_spec=pltpu.PrefetchScalarGridSpec(
            num_scalar_prefetch=2,
            grid=(n_tiles,),
            in_specs=[
                pl.BlockSpec((COMBINE_TILE, n_e), lambda i, s, c: (i, 0)),
                pl.BlockSpec((COMBINE_TILE, n_e), lambda i, s, c: (i, 0)),
                pl.BlockSpec((COMBINE_TILE, d), lambda i, s, c: (i, 0)),
                pl.BlockSpec((1, 1, d), lambda i, s, c: (i // tiles_per_seq, 0, 0)),
                pl.BlockSpec((1, d), lambda i, s, c: (0, 0)),
                pl.BlockSpec(memory_space=pl.ANY),
            ],
            out_specs=pl.BlockSpec((COMBINE_TILE, d), lambda i, s, c: (i, 0)),
            scratch_shapes=[
                pltpu.VMEM((2, d // LANES, _COMBINE_ROWS, LANES), _F32),
                pltpu.VMEM((COMBINE_TILE, d), _F32),
                pltpu.SemaphoreType.DMA((2,)),
            ],
        ),
        compiler_params=pltpu.CompilerParams(
            dimension_semantics=("arbitrary",), vmem_limit_bytes=VMEM_LIMIT),
        name="combine",
    )(start_tab, cnt_tab, pos_t, aff_tok, x1, gt2, g_final, y_exp)


def _moe(x1, h2, aff_tok, gt2, g_final, w_gate, w_up, w_down, layer, final_norm):
    bsz, s, d = x1.shape
    n_tok, n_e = aff_tok.shape
    cap = max(1, CAPACITY_FACTOR * n_tok // n_e)
    pos, start_rep = _route(aff_tok.T, cap)
    row_start = start_rep[:, :, 0]

    def tile_tables(tile):
        first = row_start[:, ::tile // LANES]
        end = jnp.concatenate([first[:, 1:], jnp.full((n_e, 1), cap, jnp.int32)], axis=1)
        return first.T.reshape(-1), (end - first).T.reshape(-1)

    x_exp = _dispatch(h2.reshape(n_tok, d), pos, *tile_tables(DISPATCH_TILE), cap)
    y_exp = _ffn(x_exp, w_gate, w_up, w_down, layer)
    x2 = _combine(y_exp, pos.reshape(n_e, n_tok).T, aff_tok, x1.reshape(n_tok, d), gt2, g_final,
                  *tile_tables(COMBINE_TILE), s, final_norm)
    return x2.reshape(bsz, s, d)


def kernel(x_prompt, x_sample, c_prompt, c_sample, w_ada, b_ada, g_norm1, g_norm2, w_in, b_in, conv_w, conv_b, conv_ln_g, conv_ln_b, sgu_ln_g, sgu_ln_b, sgu_w, sgu_b, pool_w, pool_scale, w_branch, w_out, w_router, w_e_gate, w_e_up, w_e_down, g_final):
    depth = w_ada.shape[0]
    d = x_prompt.shape[-1]
    groups = ((x_prompt, c_prompt), (x_sample, c_sample))

    n_c = sum(c.shape[0] for _, c in groups)
    pad = -n_c % SUBLANES
    c_all = jnp.concatenate([c for _, c in groups] + [jnp.zeros((pad, d), _F32)], axis=0)
    mod_all = _adaln(c_all, w_ada, b_ada)

    row = lambda v: v.reshape(1, -1)
    layers = []
    for l in range(depth):
        wr_hi = w_router[l].astype(_BF16)
        layers.append(dict(
            g1=row(g_norm1[l]), g2=row(g_norm2[l]),
            w_in=w_in[l].astype(_BF16), b_in=row(b_in[l]),
            conv_w=jnp.broadcast_to(conv_w[l][:, None, :], (CONV_WIDTH, SUBLANES, d)), conv_b=row(conv_b[l]),
            conv_ln_g=row(conv_ln_g[l]), conv_ln_b=row(conv_ln_b[l]),
            sgu_ln_g=row(sgu_ln_g[l]), sgu_ln_b=row(sgu_ln_b[l]),
            sgu_w=sgu_w[l].astype(_BF16),
            sgu_b=jnp.repeat(sgu_b[l].T, d // SGU_HEADS, axis=1),
            pool_w=pool_w[l].astype(_BF16), pool_scale=row(pool_scale[l]),
            w_branch=w_branch[l].astype(_BF16), w_out=w_out[l].astype(_BF16),
            w_router_hi=wr_hi, w_router_lo=(w_router[l] - wr_hi.astype(_F32)).astype(_BF16),
        ))
    gf = row(g_final)

    outs = []
    c0 = 0
    for x, c in groups:
        bsz = x.shape[0]
        for l in range(depth):
            mod = mod_all[l, c0:c0 + bsz].reshape(bsz, 6, d)
            x1, h2, aff_tok = _mixer(x, mod, layers[l])
            x = _moe(x1, h2, aff_tok, mod[:, 5:6, :], gf, w_e_gate, w_e_up, w_e_down, l,
                     final_norm=(l == depth - 1))
        outs.append(x)
        c0 += bsz
    return tuple(outs)
```

```python
import functools

import jax
import jax.numpy as jnp
from jax import lax
from jax.experimental import pallas as pl
from jax.experimental.pallas import tpu as pltpu

EPS = 1e-6
CONV_WIDTH = 31
CONV_PAD = CONV_WIDTH // 2
SGU_CHUNK = 128
SGU_HEADS = 8
POOL_WINDOWS = (2, 4, 8, 16)
N_BRANCH = 3
N_EXPERTS = 16
CAPACITY_FACTOR = 2

LANES = 128
SUBLANES = 8
HALO = 16
MIX_TILE = 256
DISPATCH_TILE = 256
COMBINE_TILE = 256
DISPATCH_CHUNK = 640
COMBINE_CHUNK = 768
FFN_ROWS = 1024
FFN_COLS = 1024
FFN_SUB_ROWS = 512
VMEM_LIMIT = 56 * 1024 * 1024
NOT_SELECTED = -(1 << 20)

_F32 = jnp.float32
_BF16 = jnp.bfloat16


def _dot(a, b):
    return jnp.dot(a, b, preferred_element_type=_F32)


def _sigmoid(x):
    return 1.0 / (1.0 + jnp.exp(-x))


def _rms_mod(x, scale, shift):
    return x * lax.rsqrt(jnp.mean(x * x, axis=-1, keepdims=True) + EPS) * scale + shift


def _layernorm(x, g, b):
    mu = jnp.mean(x, axis=-1, keepdims=True)
    xc = x - mu
    var = jnp.mean(xc * xc, axis=-1, keepdims=True)
    return xc * lax.rsqrt(var + EPS) * g + b


def _adaln_kernel(c_ref, w_ref, b_ref, o_ref):
    c = c_ref[...]
    s = c * _sigmoid(c)
    o_ref[0] = jnp.dot(s, w_ref[0], preferred_element_type=_F32,
                       precision=lax.Precision.HIGHEST) + b_ref[0]


def _adaln(c_all, w_ada, b_ada):
    depth, d, n = w_ada.shape
    rows = c_all.shape[0]
    return pl.pallas_call(
        _adaln_kernel,
        out_shape=jax.ShapeDtypeStruct((depth, rows, n), _F32),
        grid=(depth, n // d),
        in_specs=[
            pl.BlockSpec((rows, d), lambda l, j: (0, 0)),
            pl.BlockSpec((1, d, d), lambda l, j: (l, 0, j)),
            pl.BlockSpec((1, 1, d), lambda l, j: (l, 0, j)),
        ],
        out_specs=pl.BlockSpec((1, rows, d), lambda l, j: (l, 0, j)),
        name="adaln",
    )(c_all, w_ada, b_ada.reshape(depth, 1, n))


def _window_sum(x, w):
    n = x.shape[0]
    a = x
    span = 1
    while span < w // 2:
        a = a + pltpu.roll(a, n - span, axis=0)
        span *= 2
    return pltpu.roll(a, w // 2, axis=0) + a


def _mixer_kernel(xp_ref, x_ref, xn_ref, mod_ref, g1_ref, g2_ref, win_ref, bin_ref,
                  convw_ref, convb_ref, clng_ref, clnb_ref, slng_ref, slnb_ref, sguw_ref, sgub_ref,
                  poolw_ref, pscale_ref, wbr_ref, wout_ref, wrh_ref, wrl_ref,
                  x1_ref, h2_ref, aff_ref,
                  hbuf, gbuf, sbuf, cbuf, ybuf, *, seq_len):
    tm = x_ref.shape[1]
    d = x_ref.shape[2]
    i = pl.program_id(1)
    n_tiles = pl.num_programs(1)

    sh1 = mod_ref[0, 0:1, :]
    sc1 = mod_ref[0, 1:2, :]
    gt1 = mod_ref[0, 2:3, :]
    sh2 = mod_ref[0, 3:4, :]
    sc2 = mod_ref[0, 4:5, :]
    scale1 = g1_ref[...] * (1.0 + sc1)
    scale2 = g2_ref[...] * (1.0 + sc2)

    xm = x_ref[0]
    hbuf[0:HALO, :] = _rms_mod(xp_ref[0], scale1, sh1).astype(_BF16)
    hbuf[HALO:HALO + tm, :] = _rms_mod(xm, scale1, sh1).astype(_BF16)
    hbuf[HALO + tm:, :] = _rms_mod(xn_ref[0], scale1, sh1).astype(_BF16)

    rows = lax.broadcasted_iota(jnp.int32, (tm + 2 * HALO, 1), 0)
    valid = jnp.logical_and(jnp.logical_or(rows >= HALO, i > 0),
                            jnp.logical_or(rows < tm + HALO, i < n_tiles - 1))

    o_u = 2 * d
    o_v = 3 * d
    o_c = 4 * d
    o_g = 5 * d

    nt_cols = 256
    for n0 in range(0, d, nt_cols):
        a = _dot(hbuf[...], win_ref[:, n0:n0 + nt_cols]) + bin_ref[:, n0:n0 + nt_cols]
        gt = _dot(hbuf[...], win_ref[:, d + n0:d + n0 + nt_cols]) + bin_ref[:, d + n0:d + n0 + nt_cols]
        gbuf[:, n0:n0 + nt_cols] = jnp.where(valid, a * _sigmoid(gt), 0.0)
    n_sh = sbuf.shape[1]
    for r in range(1, SUBLANES):
        sbuf[r - 1] = gbuf[r:r + n_sh, :]
    n_grp = 4
    rc = n_grp * SUBLANES
    for c in range(tm // rc):
        r0 = c * rc
        accs = [None] * n_grp
        for k in range(CONV_WIDTH):
            q, r = divmod(HALO - CONV_PAD + k, SUBLANES)
            w8 = convw_ref[k]
            for g in range(n_grp):
                lo = r0 + (q + g) * SUBLANES
                src = gbuf[lo:lo + SUBLANES, :] if r == 0 else sbuf[r - 1, lo:lo + SUBLANES, :]
                term = src * w8
                accs[g] = term if accs[g] is None else accs[g] + term
        acc = jnp.concatenate(accs, axis=0)
        y = _layernorm(acc + convb_ref[...], clng_ref[...], clnb_ref[...])
        ybuf[0, r0:r0 + rc, :] = (y * _sigmoid(y)).astype(_BF16)

    hm = hbuf[HALO:HALO + tm, :]
    pu = _dot(hm, win_ref[:, o_u:o_v]) + bin_ref[:, o_u:o_v]
    pv = _dot(hm, win_ref[:, o_v:o_c]) + bin_ref[:, o_v:o_c]
    vn = _layernorm(pv, slng_ref[...], slnb_ref[...]).astype(_BF16)
    dh = d // SGU_HEADS
    for n in range(tm // SGU_CHUNK):
        r0 = n * SGU_CHUNK
        for hd in range(SGU_HEADS):
            c0 = hd * dh
            vm = _dot(sguw_ref[hd], vn[r0:r0 + SGU_CHUNK, c0:c0 + dh]) + sgub_ref[:, c0:c0 + dh]
            ybuf[1, r0:r0 + SGU_CHUNK, c0:c0 + dh] = (pu[r0:r0 + SGU_CHUNK, c0:c0 + dh] * vm).astype(_BF16)

    pc = _dot(hbuf[...], win_ref[:, o_c:o_g]) + bin_ref[:, o_c:o_g]
    cbuf[...] = jnp.where(valid, pc, 0.0)
    pos = i * tm + lax.broadcasted_iota(jnp.int32, (tm, 1), 0)
    gc = d // len(POOL_WINDOWS)
    for g, w in enumerate(POOL_WINDOWS):
        c0 = g * gc
        half = w // 2
        tot = _window_sum(cbuf[:, c0:c0 + gc], w)[HALO:HALO + tm]
        lo = jnp.maximum(pos - half, 0)
        hi = jnp.minimum(pos - half + w, seq_len)
        mean = tot / (hi - lo).astype(_F32)
        diff = mean - cbuf[HALO:HALO + tm, c0:c0 + gc]
        yc = _dot(diff.astype(_BF16), poolw_ref[g]) * pscale_ref[:, c0:c0 + gc]
        ybuf[2, :, c0:c0 + gc] = yc.astype(_BF16)

    parts = []
    for n0 in range(0, d, nt_cols):
        part = None
        for b in range(N_BRANCH):
            lo = o_g + b * d + n0
            gate = _sigmoid(_dot(hm, win_ref[:, lo:lo + nt_cols]) + bin_ref[:, lo:lo + nt_cols])
            term = _dot(ybuf[b], wbr_ref[b, :, n0:n0 + nt_cols]) * gate
            part = term if part is None else part + term
        parts.append(part.astype(_BF16))
    mix = _dot(jnp.concatenate(parts, axis=1), wout_ref[...])
    x1 = xm + gt1 * mix
    x1_ref[0] = x1

    h2 = _rms_mod(x1, scale2, sh2)
    h2_hi = h2.astype(_BF16)
    h2_ref[0] = h2_hi
    h2_lo = (h2 - h2_hi.astype(_F32)).astype(_BF16)
    logits = _dot(h2_hi, wrh_ref[...]) + (_dot(h2_lo, wrh_ref[...]) + _dot(h2_hi, wrl_ref[...]))
    m = jnp.max(logits, axis=1, keepdims=True)
    ex = jnp.exp(logits - m)
    aff_ref[...] = ex / jnp.sum(ex, axis=1, keepdims=True)


def _const_spec(shape):
    nd = len(shape)
    return pl.BlockSpec(shape, lambda b, i: (0,) * nd, pipeline_mode=pl.Buffered(1))


def _mixer(x, mod, lw, tm=MIX_TILE):
    bsz, s, d = x.shape
    nt = s // tm
    hb = tm // HALO
    n_in = lw["w_in"].shape[1]
    x_halo = lambda f: pl.BlockSpec((1, HALO, d), f)
    return pl.pallas_call(
        functools.partial(_mixer_kernel, seq_len=s),
        out_shape=(
            jax.ShapeDtypeStruct((bsz, s, d), _F32),
            jax.ShapeDtypeStruct((bsz, s, d), _BF16),
            jax.ShapeDtypeStruct((bsz * s, N_EXPERTS), _F32),
        ),
        grid=(bsz, nt),
        in_specs=[
            x_halo(lambda b, i: (b, jnp.maximum(i * hb - 1, 0), 0)),
            pl.BlockSpec((1, tm, d), lambda b, i: (b, i, 0)),
            x_halo(lambda b, i: (b, jnp.minimum((i + 1) * hb, s // HALO - 1), 0)),
            pl.BlockSpec((1, 6, d), lambda b, i: (b, 0, 0)),
            _const_spec((1, d)), _const_spec((1, d)),
            _const_spec((d, n_in)), _const_spec((1, n_in)),
            _const_spec((CONV_WIDTH, SUBLANES, d)), _const_spec((1, d)), _const_spec((1, d)), _const_spec((1, d)),
            _const_spec((1, d)), _const_spec((1, d)),
            _const_spec((SGU_HEADS, SGU_CHUNK, SGU_CHUNK)), _const_spec((SGU_CHUNK, d)),
            _const_spec(lw["pool_w"].shape), _const_spec((1, d)),
            _const_spec((N_BRANCH, d, d)), _const_spec((d, d)),
            _const_spec((d, N_EXPERTS)), _const_spec((d, N_EXPERTS)),
        ],
        out_specs=(
            pl.BlockSpec((1, tm, d), lambda b, i: (b, i, 0)),
            pl.BlockSpec((1, tm, d), lambda b, i: (b, i, 0)),
            pl.BlockSpec((tm, N_EXPERTS), lambda b, i: (b * nt + i, 0)),
        ),
        scratch_shapes=[
            pltpu.VMEM((tm + 2 * HALO, d), _BF16),
            pltpu.VMEM((tm + 2 * HALO, d), _F32),
            pltpu.VMEM((SUBLANES - 1, tm + 2 * HALO - SUBLANES, d), _F32),
            pltpu.VMEM((tm + 2 * HALO, d), _F32),
            pltpu.VMEM((N_BRANCH, tm, d), _BF16),
        ],
        compiler_params=pltpu.CompilerParams(
            dimension_semantics=("arbitrary", "arbitrary"), vmem_limit_bytes=VMEM_LIMIT),
        name="mixer",
    )(x, x, x, mod, lw["g1"], lw["g2"], lw["w_in"], lw["b_in"], lw["conv_w"], lw["conv_b"],
      lw["conv_ln_g"], lw["conv_ln_b"], lw["sgu_ln_g"], lw["sgu_ln_b"], lw["sgu_w"], lw["sgu_b"],
      lw["pool_w"], lw["pool_scale"], lw["w_branch"], lw["w_out"], lw["w_router_hi"], lw["w_router_lo"])


def _route_kernel(aff_ref, pos_ref, start_ref, *, cap, n_tok):
    n_e, n_rows, _ = aff_ref.shape
    keys = pltpu.bitcast(aff_ref[...], jnp.int32)

    def count(ind):
        return jnp.sum(jnp.sum(ind, axis=1, keepdims=True), axis=2, keepdims=True)

    def thr_step(it, prefix):
        cand = prefix | jnp.left_shift(jnp.int32(1), 30 - it)
        c = count(jnp.where(keys >= cand, 1.0, 0.0))
        return jnp.where(c >= cap, cand, prefix)

    thr = lax.fori_loop(0, 31, thr_step, jnp.zeros((n_e, 1, 1), jnp.int32))
    gt = keys > thr
    eq = keys == thr
    need = cap - count(jnp.where(gt, 1.0, 0.0))
    tidx = (lax.broadcasted_iota(jnp.int32, (1, n_rows, LANES), 1) * LANES
            + lax.broadcasted_iota(jnp.int32, (1, n_rows, LANES), 2))
    n_bits = max(1, (n_tok - 1).bit_length())

    def tie_step(it, bound):
        cand = bound + jnp.left_shift(jnp.int32(1), n_bits - 1 - it)
        c = count(jnp.where(eq, jnp.where(tidx < cand, 1.0, 0.0), 0.0))
        return jnp.where(c < need, cand, bound)

    bound = lax.fori_loop(0, n_bits, tie_step, jnp.zeros((n_e, 1, 1), jnp.int32))
    sel = jnp.where(gt, 1.0, jnp.where(eq, jnp.where(tidx <= bound, 1.0, 0.0), 0.0))

    r_i = lax.broadcasted_iota(jnp.int32, (LANES, LANES), 0)
    c_i = lax.broadcasted_iota(jnp.int32, (LANES, LANES), 1)
    upper = jnp.where(r_i <= c_i, 1.0, 0.0).astype(_BF16)
    ones = jnp.ones((LANES, LANES), _BF16)
    rr = lax.broadcasted_iota(jnp.int32, (n_rows, n_rows), 0)
    rc = lax.broadcasted_iota(jnp.int32, (n_rows, n_rows), 1)
    lower = jnp.where(rc < rr, 1.0, 0.0).astype(_BF16)
    for e in range(n_e):
        s = sel[e]
        sb = s.astype(_BF16)
        incl = _dot(sb, upper)
        tot = _dot(sb, ones)
        start = _dot(lower, tot.astype(_BF16))
        slot = (start + incl).astype(jnp.int32) - 1
        pos_ref[e] = jnp.where(s > 0.5, slot, NOT_SELECTED)
        start_ref[e] = start.astype(jnp.int32)


def _route(aff_t, cap):
    n_e, n_tok = aff_t.shape
    n_rows = n_tok // LANES
    return pl.pallas_call(
        functools.partial(_route_kernel, cap=cap, n_tok=n_tok),
        out_shape=(
            jax.ShapeDtypeStruct((n_e, n_rows, LANES), jnp.int32),
            jax.ShapeDtypeStruct((n_e, n_rows, LANES), jnp.int32),
        ),
        name="route",
    )(aff_t.reshape(n_e, n_rows, LANES))


def _round_up(n, m):
    return -(-n // m) * m


_DISPATCH_ROWS = _round_up(N_EXPERTS * DISPATCH_TILE, DISPATCH_CHUNK)
_COL_BITS = 7
_COL_SPLIT = 1 << _COL_BITS
_NO_COLUMN = _COL_SPLIT * _COL_SPLIT - 1
assert _DISPATCH_ROWS <= _NO_COLUMN


def _dispatch_copies(start_ref, cnt_ref, xbuf, x_hbm, sems, tile, slot, act):
    run = 0
    for e in range(N_EXPERTS):
        cnt = cnt_ref[tile * N_EXPERTS + e]
        st = start_ref[tile * N_EXPERTS + e]

        @pl.when(cnt > 0)
        def _(run=run, cnt=cnt, st=st, e=e):
            src = xbuf.at[slot, pl.ds(pl.multiple_of(run * SUBLANES, SUBLANES), cnt * SUBLANES)]
            dst = x_hbm.at[e, pl.ds(pl.multiple_of(st * SUBLANES, SUBLANES), cnt * SUBLANES)]
            act(pltpu.make_async_copy(src, dst, sems.at[slot]))
        run = run + cnt


def _dispatch_kernel(start_ref, cnt_ref, pos_ref, h2_ref, x_hbm, xbuf, sems):
    i = pl.program_id(0)
    last = pl.num_programs(0) - 1
    slot = i % 2
    tt, d = h2_ref.shape

    n_e = N_EXPERTS
    ch = DISPATCH_CHUNK

    row_e = lax.broadcasted_iota(jnp.int32, (n_e, 1), 0)
    shift = jnp.zeros((n_e, 1), jnp.int32)
    ends = []
    run = 0
    for e in range(n_e):
        shift = jnp.where(row_e == e, run - start_ref[i * n_e + e], shift)
        run = run + cnt_ref[i * n_e + e]
        ends.append(run)
    n_rows = run
    pos = pos_ref[0]
    col = jnp.where(pos >= 0, pos + shift, _NO_COLUMN)
    col_hi = (col >> _COL_BITS).astype(_F32).astype(_BF16)
    col_lo = (col & (_COL_SPLIT - 1)).astype(_F32).astype(_BF16)

    for k in range(_DISPATCH_ROWS // ch):
        @pl.when(k * ch < n_rows)
        def _(k=k):
            c_i = k * ch + lax.broadcasted_iota(jnp.int32, (1, ch), 1)
            owner = jnp.zeros((1, ch), jnp.int32)
            for end in ends:
                owner = owner + jnp.where(c_i >= end, 1, 0)
            spread = jnp.where(lax.broadcasted_iota(jnp.int32, (n_e, ch), 0) == owner, 1.0, 0.0).astype(_BF16)
            down = lambda a: lax.dot_general(spread, a, (((0,), (0,)), ((), ())), preferred_element_type=_F32)
            row_c = _COL_SPLIT * down(col_hi) + down(col_lo)
            c_col = (k * ch + lax.broadcasted_iota(jnp.int32, (ch, tt), 0)).astype(_F32)
            ptb = jnp.where(row_c == c_col, 1.0, 0.0).astype(_BF16)
            for j2 in range(d // (2 * LANES)):
                r = _dot(ptb, h2_ref[:, j2 * 2 * LANES:(j2 + 1) * 2 * LANES])
                for h in range(2):
                    rows = pl.ds(k * ch * SUBLANES + 2 * j2 + h, ch, stride=SUBLANES)
                    xbuf[slot, rows, :] = r[:, h * LANES:(h + 1) * LANES]

    @pl.when(i > 0)
    def _():
        _dispatch_copies(start_ref, cnt_ref, xbuf, x_hbm, sems, i - 1, 1 - slot, lambda cp: cp.wait())

    _dispatch_copies(start_ref, cnt_ref, xbuf, x_hbm, sems, i, slot, lambda cp: cp.start())

    @pl.when(i == last)
    def _():
        _dispatch_copies(start_ref, cnt_ref, xbuf, x_hbm, sems, i, slot, lambda cp: cp.wait())


def _dispatch(h2, pos, start_tab, cnt_tab, cap):
    n_tok, d = h2.shape
    n_e, n_rows, _ = pos.shape
    assert d == SUBLANES * LANES, "one activation row must fill one 8 x 128 tile"
    n_tiles = n_tok // DISPATCH_TILE
    return pl.pallas_call(
        _dispatch_kernel,
        out_shape=jax.ShapeDtypeStruct((n_e, cap * SUBLANES, LANES), _F32),
        grid_spec=pltpu.PrefetchScalarGridSpec(
            num_scalar_prefetch=2,
            grid=(n_tiles,),
            in_specs=[
                pl.BlockSpec((1, n_e, DISPATCH_TILE), lambda i, s, c: (i, 0, 0)),
                pl.BlockSpec((DISPATCH_TILE, d), lambda i, s, c: (i, 0)),
            ],
            out_specs=pl.BlockSpec(memory_space=pl.ANY),
            scratch_shapes=[
                pltpu.VMEM((2, _DISPATCH_ROWS * SUBLANES, LANES), _F32),
                pltpu.SemaphoreType.DMA((2,)),
            ],
        ),
        compiler_params=pltpu.CompilerParams(
            dimension_semantics=("arbitrary",), vmem_limit_bytes=VMEM_LIMIT),
        name="dispatch",
    )(start_tab, cnt_tab, pos.reshape(n_e, n_tiles, DISPATCH_TILE).transpose(1, 0, 2), h2)


def _ffn_kernel(x_ref, wg_ref, wu_ref, wd_ref, y_ref, xb):
    f = pl.program_id(2)
    n_j = SUBLANES
    mt = xb.shape[0]

    @pl.when(f == 0)
    def _():
        for j in range(n_j):
            xb[:, j * LANES:(j + 1) * LANES] = x_ref[0, pl.ds(j, mt, stride=SUBLANES), :].astype(_BF16)

    def body(first):
        wg = wg_ref[0, 0].astype(_BF16)
        wu = wu_ref[0, 0].astype(_BF16)
        wd = wd_ref[0, 0].astype(_BF16)
        rb = min(FFN_SUB_ROWS, mt)
        for s in range(mt // rb):
            x = xb[s * rb:(s + 1) * rb, :]
            g = _dot(x, wg)
            u = _dot(x, wu)
            hid = (g * _sigmoid(g) * u).astype(_BF16)
            res = _dot(hid, wd)
            for j in range(n_j):
                part = res[:, j * LANES:(j + 1) * LANES]
                if first:
                    y_ref[0, j, s * rb:(s + 1) * rb, :] = part
                else:
                    y_ref[0, j, s * rb:(s + 1) * rb, :] += part

    pl.when(f == 0)(functools.partial(body, True))
    pl.when(f > 0)(functools.partial(body, False))


def _ffn(x_exp, w_gate, w_up, w_down, layer):
    n_e = x_exp.shape[0]
    n_j = SUBLANES
    cap = x_exp.shape[1] // n_j
    d = n_j * LANES
    d_exp = w_gate.shape[-1]
    mt = min(FFN_ROWS, cap)
    fc = min(FFN_COLS, d_exp)
    return pl.pallas_call(
        _ffn_kernel,
        out_shape=jax.ShapeDtypeStruct((n_e, n_j, cap, LANES), _F32),
        grid=(n_e, cap // mt, d_exp // fc),
        in_specs=[
            pl.BlockSpec((1, mt * n_j, LANES), lambda e, m, f: (e, m, 0)),
            pl.BlockSpec((1, 1, d, fc), lambda e, m, f: (layer, e, 0, f)),
            pl.BlockSpec((1, 1, d, fc), lambda e, m, f: (layer, e, 0, f)),
            pl.BlockSpec((1, 1, fc, d), lambda e, m, f: (layer, e, f, 0)),
        ],
        out_specs=pl.BlockSpec((1, n_j, mt, LANES), lambda e, m, f: (e, 0, m, 0)),
        scratch_shapes=[pltpu.VMEM((mt, d), _BF16)],
        compiler_params=pltpu.CompilerParams(
            dimension_semantics=("arbitrary", "arbitrary", "arbitrary"), vmem_limit_bytes=VMEM_LIMIT),
        name="ffn",
    )(x_exp, w_gate, w_up, w_down)


_COMBINE_ROWS = _round_up(N_EXPERTS * (COMBINE_TILE + 2 * SUBLANES), COMBINE_CHUNK)
assert _COMBINE_ROWS <= _NO_COLUMN


def _combine_windows(start_ref, cnt_ref, tile):
    out = []
    run = 0
    for e in range(N_EXPERTS):
        cnt = cnt_ref[tile * N_EXPERTS + e]
        st = start_ref[tile * N_EXPERTS + e]
        a0 = (st >> 3) << 3
        a1 = ((st + cnt + SUBLANES - 1) >> 3) << 3
        rows = jnp.where(cnt > 0, a1 - a0, 0)
        out.append((a0, rows, run))
        run = run + rows
    return out, run


def _combine_copies(start_ref, cnt_ref, y_hbm, ybuf, sems, tile, slot, act):
    wins, _ = _combine_windows(start_ref, cnt_ref, tile)
    for e, (a0, rows, boff) in enumerate(wins):
        @pl.when(rows > 0)
        def _(a0=a0, rows=rows, boff=boff, e=e):
            n = pl.multiple_of(rows, SUBLANES)
            src = y_hbm.at[e, :, pl.ds(pl.multiple_of(a0, SUBLANES), n), :]
            dst = ybuf.at[slot, :, pl.ds(pl.multiple_of(boff, SUBLANES), n), :]
            act(pltpu.make_async_copy(src, dst, sems.at[slot]))


def _combine_kernel(start_ref, cnt_ref, pos_ref, aff_ref, x1_ref, gt_ref, gf_ref, y_hbm, o_ref,
                    ybuf, acc, sems, *, final_norm):
    i = pl.program_id(0)
    n_tiles = pl.num_programs(0)
    slot = i % 2
    tt, d = x1_ref.shape

    @pl.when(i == 0)
    def _():
        ybuf[...] = jnp.zeros_like(ybuf)
        _combine_copies(start_ref, cnt_ref, y_hbm, ybuf, sems, i, slot, lambda cp: cp.start())

    @pl.when(i + 1 < n_tiles)
    def _():
        _combine_copies(start_ref, cnt_ref, y_hbm, ybuf, sems, i + 1, 1 - slot, lambda cp: cp.start())

    _combine_copies(start_ref, cnt_ref, y_hbm, ybuf, sems, i, slot, lambda cp: cp.wait())

    wins, n_rows = _combine_windows(start_ref, cnt_ref, i)
    n_e = pos_ref.shape[1]
    ch = COMBINE_CHUNK

    lane_e = lax.broadcasted_iota(jnp.int32, (1, n_e), 1)
    shift = jnp.zeros((1, n_e), jnp.int32)
    for e, (a0, _, boff) in enumerate(wins):
        shift = jnp.where(lane_e == e, boff - a0, shift)
    pos = pos_ref[...]
    col = jnp.where(pos >= 0, pos + shift, _NO_COLUMN)
    col_hi = (col >> _COL_BITS).astype(_F32).astype(_BF16)
    col_lo = (col & (_COL_SPLIT - 1)).astype(_F32).astype(_BF16)
    aff = aff_ref[...]
    a_hi = aff.astype(_BF16)
    a_lo = (aff - a_hi.astype(_F32)).astype(_BF16)

    acc[...] = jnp.zeros_like(acc)
    for k in range(_COMBINE_ROWS // ch):
        @pl.when(k * ch < n_rows)
        def _(k=k):
            c_i = k * ch + lax.broadcasted_iota(jnp.int32, (1, ch), 1)
            owner = jnp.zeros((1, ch), jnp.int32)
            for _, rows, boff in wins:
                owner = owner + jnp.where(c_i >= boff + rows, 1, 0)
            spread = jnp.where(lax.broadcasted_iota(jnp.int32, (n_e, ch), 0) == owner, 1.0, 0.0).astype(_BF16)
            col_c = _COL_SPLIT * _dot(col_hi, spread) + _dot(col_lo, spread)
            hit = col_c == c_i.astype(_F32)
            ph = jnp.where(hit, _dot(a_hi, spread), 0.0).astype(_BF16)
            plo = jnp.where(hit, _dot(a_lo, spread), 0.0).astype(_BF16)
            for j2 in range(d // (2 * LANES)):
                y = jnp.concatenate([ybuf[slot, 2 * j2, pl.ds(k * ch, ch), :],
                                     ybuf[slot, 2 * j2 + 1, pl.ds(k * ch, ch), :]], axis=1)
                yh = y.astype(_BF16)
                ylo = (y - yh.astype(_F32)).astype(_BF16)
                acc[:, j2 * 2 * LANES:(j2 + 1) * 2 * LANES] += _dot(ph, yh) + (_dot(ph, ylo) + _dot(plo, yh))

    x2 = x1_ref[...] + gt_ref[0] * acc[...]
    if final_norm:
        x2 = x2 * lax.rsqrt(jnp.mean(x2 * x2, axis=-1, keepdims=True) + EPS) * gf_ref[...]
    o_ref[...] = x2


def _combine(y_exp, pos_t, aff_tok, x1, gt2, g_final, start_tab, cnt_tab, seq_len, final_norm):
    n_tok, d = x1.shape
    n_e = y_exp.shape[0]
    n_tiles = n_tok // COMBINE_TILE
    tiles_per_seq = seq_len // COMBINE_TILE
    return pl.pallas_call(
        functools.partial(_combine_kernel, final_norm=final_norm),
        out_shape=jax.ShapeDtypeStruct((n_tok, d), _F32),
        grid_spec=pltpu.PrefetchScalarGridSpec(
            num_scalar_prefetch=2,
            grid=(n_tiles,),
            in_specs=[
                pl.BlockSpec((COMBINE_TILE, n_e), lambda i, s, c: (i, 0)),
                pl.BlockSpec((COMBINE_TILE, n_e), lambda i, s, c: (i, 0)),
                pl.BlockSpec((COMBINE_TILE, d), lambda i, s, c: (i, 0)),
                pl.BlockSpec((1, 1, d), lambda i, s, c: (i // tiles_per_seq, 0, 0)),
                pl.BlockSpec((1, d), lambda i, s, c: (0, 0)),
                pl.BlockSpec(memory_space=pl.ANY),
            ],
            out_specs=pl.BlockSpec((COMBINE_TILE, d), lambda i, s, c: (i, 0)),
            scratch_shapes=[
                pltpu.VMEM((2, d // LANES, _COMBINE_ROWS, LANES), _F32),
                pltpu.VMEM((COMBINE_TILE, d), _F32),
                pltpu.SemaphoreType.DMA((2,)),
            ],
        ),
        compiler_params=pltpu.CompilerParams(
            dimension_semantics=("arbitrary",), vmem_limit_bytes=VMEM_LIMIT),
        name="combine",
    )(start_tab, cnt_tab, pos_t, aff_tok, x1, gt2, g_final, y_exp)


def _moe(x1, h2, aff_tok, gt2, g_final, w_gate, w_up, w_down, layer, final_norm):
    bsz, s, d = x1.shape
    n_tok, n_e = aff_tok.shape
    cap = max(1, CAPACITY_FACTOR * n_tok // n_e)
    pos, start_rep = _route(aff_tok.T, cap)
    row_start = start_rep[:, :, 0]

    def tile_tables(tile):
        first = row_start[:, ::tile // LANES]
        end = jnp.concatenate([first[:, 1:], jnp.full((n_e, 1), cap, jnp.int32)], axis=1)
        return first.T.reshape(-1), (end - first).T.reshape(-1)

    x_exp = _dispatch(h2.reshape(n_tok, d), pos, *tile_tables(DISPATCH_TILE), cap)
    y_exp = _ffn(x_exp, w_gate, w_up, w_down, layer)
    x2 = _combine(y_exp, pos.reshape(n_e, n_tok).T, aff_tok, x1.reshape(n_tok, d), gt2, g_final,
                  *tile_tables(COMBINE_TILE), s, final_norm)
    return x2.reshape(bsz, s, d)


def kernel(x_prompt, x_sample, c_prompt, c_sample, w_ada, b_ada, g_norm1, g_norm2, w_in, b_in, conv_w, conv_b, conv_ln_g, conv_ln_b, sgu_ln_g, sgu_ln_b, sgu_w, sgu_b, pool_w, pool_scale, w_branch, w_out, w_router, w_e_gate, w_e_up, w_e_down, g_final):
    depth = w_ada.shape[0]
    d = x_prompt.shape[-1]
    groups = ((x_prompt, c_prompt), (x_sample, c_sample))

    n_c = sum(c.shape[0] for _, c in groups)
    pad = -n_c % SUBLANES
    c_all = jnp.concatenate([c for _, c in groups] + [jnp.zeros((pad, d), _F32)], axis=0)
    mod_all = _adaln(c_all, w_ada, b_ada)

    row = lambda v: v.reshape(1, -1)
    layers = []
    for l in range(depth):
        wr_hi = w_router[l].astype(_BF16)
        layers.append(dict(
            g1=row(g_norm1[l]), g2=row(g_norm2[l]),
            w_in=w_in[l].astype(_BF16), b_in=row(b_in[l]),
            conv_w=jnp.broadcast_to(conv_w[l][:, None, :], (CONV_WIDTH, SUBLANES, d)), conv_b=row(conv_b[l]),
            conv_ln_g=row(conv_ln_g[l]), conv_ln_b=row(conv_ln_b[l]),
            sgu_ln_g=row(sgu_ln_g[l]), sgu_ln_b=row(sgu_ln_b[l]),
            sgu_w=sgu_w[l].astype(_BF16),
            sgu_b=jnp.repeat(sgu_b[l].T, d // SGU_HEADS, axis=1),
            pool_w=pool_w[l].astype(_BF16), pool_scale=row(pool_scale[l]),
            w_branch=w_branch[l].astype(_BF16), w_out=w_out[l].astype(_BF16),
            w_router_hi=wr_hi, w_router_lo=(w_router[l] - wr_hi.astype(_F32)).astype(_BF16),
        ))
    gf = row(g_final)

    outs = []
    c0 = 0
    for x, c in groups:
        bsz = x.shape[0]
        for l in range(depth):
            mod = mod_all[l, c0:c0 + bsz].reshape(bsz, 6, d)
            x1, h2, aff_tok = _mixer(x, mod, layers[l])
            x = _moe(x1, h2, aff_tok, mod[:, 5:6, :], gf, w_e_gate, w_e_up, w_e_down, l,
                     final_norm=(l == depth - 1))
        outs.append(x)
        c0 += bsz
    return tuple(outs)
```

```python
import functools

import jax
import jax.numpy as jnp
from jax import lax
from jax.experimental import pallas as pl
from jax.experimental.pallas import tpu as pltpu

EPS = 1e-6
CONV_WIDTH = 31
CONV_PAD = CONV_WIDTH // 2
SGU_CHUNK = 128
SGU_HEADS = 8
POOL_WINDOWS = (2, 4, 8, 16)
N_BRANCH = 3
N_EXPERTS = 16
CAPACITY_FACTOR = 2

LANES = 128
SUBLANES = 8
HALO = 16
MIX_TILE = 256
DISPATCH_TILE = 256
COMBINE_TILE = 256
DISPATCH_CHUNK = 640
COMBINE_CHUNK = 768
FFN_ROWS = 1024
FFN_COLS = 1024
FFN_SUB_ROWS = 512
VMEM_LIMIT = 56 * 1024 * 1024
NOT_SELECTED = -(1 << 20)

_F32 = jnp.float32
_BF16 = jnp.bfloat16


def _dot(a, b):
    return jnp.dot(a, b, preferred_element_type=_F32)


def _sigmoid(x):
    return 1.0 / (1.0 + jnp.exp(-x))


def _rms_mod(x, scale, shift):
    return x * lax.rsqrt(jnp.mean(x * x, axis=-1, keepdims=True) + EPS) * scale + shift


def _layernorm(x, g, b):
    mu = jnp.mean(x, axis=-1, keepdims=True)
    xc = x - mu
    var = jnp.mean(xc * xc, axis=-1, keepdims=True)
    return xc * lax.rsqrt(var + EPS) * g + b


def _adaln_kernel(c_ref, w_ref, b_ref, o_ref):
    c = c_ref[...]
    s = c * _sigmoid(c)
    o_ref[0] = jnp.dot(s, w_ref[0], preferred_element_type=_F32,
                       precision=lax.Precision.HIGHEST) + b_ref[0]


def _adaln(c_all, w_ada, b_ada):
    depth, d, n = w_ada.shape
    rows = c_all.shape[0]
    return pl.pallas_call(
        _adaln_kernel,
        out_shape=jax.ShapeDtypeStruct((depth, rows, n), _F32),
        grid=(depth, n // d),
        in_specs=[
            pl.BlockSpec((rows, d), lambda l, j: (0, 0)),
            pl.BlockSpec((1, d, d), lambda l, j: (l, 0, j)),
            pl.BlockSpec((1, 1, d), lambda l, j: (l, 0, j)),
        ],
        out_specs=pl.BlockSpec((1, rows, d), lambda l, j: (l, 0, j)),
        name="adaln",
    )(c_all, w_ada, b_ada.reshape(depth, 1, n))


def _window_sum(x, w):
    n = x.shape[0]
    a = x
    span = 1
    while span < w // 2:
        a = a + pltpu.roll(a, n - span, axis=0)
        span *= 2
    return pltpu.roll(a, w // 2, axis=0) + a


def _mixer_kernel(xp_ref, x_ref, xn_ref, mod_ref, g1_ref, g2_ref, win_ref, bin_ref,
                  convw_ref, convb_ref, clng_ref, clnb_ref, slng_ref, slnb_ref, sguw_ref, sgub_ref,
                  poolw_ref, pscale_ref, wbr_ref, wout_ref, wrh_ref, wrl_ref,
                  x1_ref, h2_ref, aff_ref,
                  hbuf, gbuf, sbuf, cbuf, ybuf, *, seq_len):
    tm = x_ref.shape[1]
    d = x_ref.shape[2]
    i = pl.program_id(1)
    n_tiles = pl.num_programs(1)

    sh1 = mod_ref[0, 0:1, :]
    sc1 = mod_ref[0, 1:2, :]
    gt1 = mod_ref[0, 2:3, :]
    sh2 = mod_ref[0, 3:4, :]
    sc2 = mod_ref[0, 4:5, :]
    scale1 = g1_ref[...] * (1.0 + sc1)
    scale2 = g2_ref[...] * (1.0 + sc2)

    xm = x_ref[0]
    hbuf[0:HALO, :] = _rms_mod(xp_ref[0], scale1, sh1).astype(_BF16)
    hbuf[HALO:HALO + tm, :] = _rms_mod(xm, scale1, sh1).astype(_BF16)
    hbuf[HALO + tm:, :] = _rms_mod(xn_ref[0], scale1, sh1).astype(_BF16)

    rows = lax.broadcasted_iota(jnp.int32, (tm + 2 * HALO, 1), 0)
    valid = jnp.logical_and(jnp.logical_or(rows >= HALO, i > 0),
                            jnp.logical_or(rows < tm + HALO, i < n_tiles - 1))

    o_u = 2 * d
    o_v = 3 * d
    o_c = 4 * d
    o_g = 5 * d

    nt_cols = 256
    for n0 in range(0, d, nt_cols):
        a = _dot(hbuf[...], win_ref[:, n0:n0 + nt_cols]) + bin_ref[:, n0:n0 + nt_cols]
        gt = _dot(hbuf[...], win_ref[:, d + n0:d + n0 + nt_cols]) + bin_ref[:, d + n0:d + n0 + nt_cols]
        gbuf[:, n0:n0 + nt_cols] = jnp.where(valid, a * _sigmoid(gt), 0.0)
    n_sh = sbuf.shape[1]
    for r in range(1, SUBLANES):
        sbuf[r - 1] = gbuf[r:r + n_sh, :]
    n_grp = 4
    rc = n_grp * SUBLANES
    for c in range(tm // rc):
        r0 = c * rc
        accs = [None] * n_grp
        for k in range(CONV_WIDTH):
            q, r = divmod(HALO - CONV_PAD + k, SUBLANES)
            w8 = convw_ref[k]
            for g in range(n_grp):
                lo = r0 + (q + g) * SUBLANES
                src = gbuf[lo:lo + SUBLANES, :] if r == 0 else sbuf[r - 1, lo:lo + SUBLANES, :]
                term = src * w8
                accs[g] = term if accs[g] is None else accs[g] + term
        acc = jnp.concatenate(accs, axis=0)
        y = _layernorm(acc + convb_ref[...], clng_ref[...], clnb_ref[...])
        ybuf[0, r0:r0 + rc, :] = (y * _sigmoid(y)).astype(_BF16)

    hm = hbuf[HALO:HALO + tm, :]
    pu = _dot(hm, win_ref[:, o_u:o_v]) + bin_ref[:, o_u:o_v]
    pv = _dot(hm, win_ref[:, o_v:o_c]) + bin_ref[:, o_v:o_c]
    vn = _layernorm(pv, slng_ref[...], slnb_ref[...]).astype(_BF16)
    dh = d // SGU_HEADS
    for n in range(tm // SGU_CHUNK):
        r0 = n * SGU_CHUNK
        for hd in range(SGU_HEADS):
            c0 = hd * dh
            vm = _dot(sguw_ref[hd], vn[r0:r0 + SGU_CHUNK, c0:c0 + dh]) + sgub_ref[:, c0:c0 + dh]
            ybuf[1, r0:r0 + SGU_CHUNK, c0:c0 + dh] = (pu[r0:r0 + SGU_CHUNK, c0:c0 + dh] * vm).astype(_BF16)

    pc = _dot(hbuf[...], win_ref[:, o_c:o_g]) + bin_ref[:, o_c:o_g]
    cbuf[...] = jnp.where(valid, pc, 0.0)
    pos = i * tm + lax.broadcasted_iota(jnp.int32, (tm, 1), 0)
    gc = d // len(POOL_WINDOWS)
    for g, w in enumerate(POOL_WINDOWS):
        c0 = g * gc
        half = w // 2
        tot = _window_sum(cbuf[:, c0:c0 + gc], w)[HALO:HALO + tm]
        lo = jnp.maximum(pos - half, 0)
        hi = jnp.minimum(pos - half + w, seq_len)
        mean = tot / (hi - lo).astype(_F32)
        diff = mean - cbuf[HALO:HALO + tm, c0:c0 + gc]
        yc = _dot(diff.astype(_BF16), poolw_ref[g]) * pscale_ref[:, c0:c0 + gc]
        ybuf[2, :, c0:c0 + gc] = yc.astype(_BF16)

    parts = []
    for n0 in range(0, d, nt_cols):
        part = None
        for b in range(N_BRANCH):
            lo = o_g + b * d + n0
            gate = _sigmoid(_dot(hm, win_ref[:, lo:lo + nt_cols]) + bin_ref[:, lo:lo + nt_cols])
            term = _dot(ybuf[b], wbr_ref[b, :, n0:n0 + nt_cols]) * gate
            part = term if part is None else part + term
        parts.append(part.astype(_BF16))
    mix = _dot(jnp.concatenate(parts, axis=1), wout_ref[...])
    x1 = xm + gt1 * mix
    x1_ref[0] = x1

    h2 = _rms_mod(x1, scale2, sh2)
    h2_hi = h2.astype(_BF16)
    h2_ref[0] = h2_hi
    h2_lo = (h2 - h2_hi.astype(_F32)).astype(_BF16)
    logits = _dot(h2_hi, wrh_ref[...]) + (_dot(h2_lo, wrh_ref[...]) + _dot(h2_hi, wrl_ref[...]))
    m = jnp.max(logits, axis=1, keepdims=True)
    ex = jnp.exp(logits - m)
    aff_ref[...] = ex / jnp.sum(ex, axis=1, keepdims=True)


def _const_spec(shape):
    nd = len(shape)
    return pl.BlockSpec(shape, lambda b, i: (0,) * nd, pipeline_mode=pl.Buffered(1))


def _mixer(x, mod, lw, tm=MIX_TILE):
    bsz, s, d = x.shape
    nt = s // tm
    hb = tm // HALO
    n_in = lw["w_in"].shape[1]
    x_halo = lambda f: pl.BlockSpec((1, HALO, d), f)
    return pl.pallas_call(
        functools.partial(_mixer_kernel, seq_len=s),
        out_shape=(
            jax.ShapeDtypeStruct((bsz, s, d), _F32),
            jax.ShapeDtypeStruct((bsz, s, d), _BF16),
            jax.ShapeDtypeStruct((bsz * s, N_EXPERTS), _F32),
        ),
        grid=(bsz, nt),
        in_specs=[
            x_halo(lambda b, i: (b, jnp.maximum(i * hb - 1, 0), 0)),
            pl.BlockSpec((1, tm, d), lambda b, i: (b, i, 0)),
            x_halo(lambda b, i: (b, jnp.minimum((i + 1) * hb, s // HALO - 1), 0)),
            pl.BlockSpec((1, 6, d), lambda b, i: (b, 0, 0)),
            _const_spec((1, d)), _const_spec((1, d)),
            _const_spec((d, n_in)), _const_spec((1, n_in)),
            _const_spec((CONV_WIDTH, SUBLANES, d)), _const_spec((1, d)), _const_spec((1, d)), _const_spec((1, d)),
            _const_spec((1, d)), _const_spec((1, d)),
            _const_spec((SGU_HEADS, SGU_CHUNK, SGU_CHUNK)), _const_spec((SGU_CHUNK, d)),
            _const_spec(lw["pool_w"].shape), _const_spec((1, d)),
            _const_spec((N_BRANCH, d, d)), _const_spec((d, d)),
            _const_spec((d, N_EXPERTS)), _const_spec((d, N_EXPERTS)),
        ],
        out_specs=(
            pl.BlockSpec((1, tm, d), lambda b, i: (b, i, 0)),
            pl.BlockSpec((1, tm, d), lambda b, i: (b, i, 0)),
            pl.BlockSpec((tm, N_EXPERTS), lambda b, i: (b * nt + i, 0)),
        ),
        scratch_shapes=[
            pltpu.VMEM((tm + 2 * HALO, d), _BF16),
            pltpu.VMEM((tm + 2 * HALO, d), _F32),
            pltpu.VMEM((SUBLANES - 1, tm + 2 * HALO - SUBLANES, d), _F32),
            pltpu.VMEM((tm + 2 * HALO, d), _F32),
            pltpu.VMEM((N_BRANCH, tm, d), _BF16),
        ],
        compiler_params=pltpu.CompilerParams(
            dimension_semantics=("arbitrary", "arbitrary"), vmem_limit_bytes=VMEM_LIMIT),
        name="mixer",
    )(x, x, x, mod, lw["g1"], lw["g2"], lw["w_in"], lw["b_in"], lw["conv_w"], lw["conv_b"],
      lw["conv_ln_g"], lw["conv_ln_b"], lw["sgu_ln_g"], lw["sgu_ln_b"], lw["sgu_w"], lw["sgu_b"],
      lw["pool_w"], lw["pool_scale"], lw["w_branch"], lw["w_out"], lw["w_router_hi"], lw["w_router_lo"])


def _route_kernel(aff_ref, pos_ref, start_ref, *, cap, n_tok):
    n_e, n_rows, _ = aff_ref.shape
    keys = pltpu.bitcast(aff_ref[...], jnp.int32)

    def count(ind):
        return jnp.sum(jnp.sum(ind, axis=1, keepdims=True), axis=2, keepdims=True)

    def thr_step(it, prefix):
        cand = prefix | jnp.left_shift(jnp.int32(1), 30 - it)
        c = count(jnp.where(keys >= cand, 1.0, 0.0))
        return jnp.where(c >= cap, cand, prefix)

    thr = lax.fori_loop(0, 31, thr_step, jnp.zeros((n_e, 1, 1), jnp.int32))
    gt = keys > thr
    eq = keys == thr
    need = cap - count(jnp.where(gt, 1.0, 0.0))
    tidx = (lax.broadcasted_iota(jnp.int32, (1, n_rows, LANES), 1) * LANES
            + lax.broadcasted_iota(jnp.int32, (1, n_rows, LANES), 2))
    n_bits = max(1, (n_tok - 1).bit_length())

    def tie_step(it, bound):
        cand = bound + jnp.left_shift(jnp.int32(1), n_bits - 1 - it)
        c = count(jnp.where(eq, jnp.where(tidx < cand, 1.0, 0.0), 0.0))
        return jnp.where(c < need, cand, bound)

    bound = lax.fori_loop(0, n_bits, tie_step, jnp.zeros((n_e, 1, 1), jnp.int32))
    sel = jnp.where(gt, 1.0, jnp.where(eq, jnp.where(tidx <= bound, 1.0, 0.0), 0.0))

    r_i = lax.broadcasted_iota(jnp.int32, (LANES, LANES), 0)
    c_i = lax.broadcasted_iota(jnp.int32, (LANES, LANES), 1)
    upper = jnp.where(r_i <= c_i, 1.0, 0.0).astype(_BF16)
    ones = jnp.ones((LANES, LANES), _BF16)
    rr = lax.broadcasted_iota(jnp.int32, (n_rows, n_rows), 0)
    rc = lax.broadcasted_iota(jnp.int32, (n_rows, n_rows), 1)
    lower = jnp.where(rc < rr, 1.0, 0.0).astype(_BF16)
    for e in range(n_e):
        s = sel[e]
        sb = s.astype(_BF16)
        incl = _dot(sb, upper)
        tot = _dot(sb, ones)
        start = _dot(lower, tot.astype(_BF16))
        slot = (start + incl).astype(jnp.int32) - 1
        pos_ref[e] = jnp.where(s > 0.5, slot, NOT_SELECTED)
        start_ref[e] = start.astype(jnp.int32)


def _route(aff_t, cap):
    n_e, n_tok = aff_t.shape
    n_rows = n_tok // LANES
    return pl.pallas_call(
        functools.partial(_route_kernel, cap=cap, n_tok=n_tok),
        out_shape=(
            jax.ShapeDtypeStruct((n_e, n_rows, LANES), jnp.int32),
            jax.ShapeDtypeStruct((n_e, n_rows, LANES), jnp.int32),
        ),
        name="route",
    )(aff_t.reshape(n_e, n_rows, LANES))


def _round_up(n, m):
    return -(-n // m) * m


_DISPATCH_ROWS = _round_up(N_EXPERTS * DISPATCH_TILE, DISPATCH_CHUNK)
_COL_BITS = 7
_COL_SPLIT = 1 << _COL_BITS
_NO_COLUMN = _COL_SPLIT * _COL_SPLIT - 1
assert _DISPATCH_ROWS <= _NO_COLUMN


def _start_copy(cp, e):
    cp.start(priority=e % 2)


def _wait_copy(cp, e):
    del e
    cp.wait()


def _dispatch_copies(start_ref, cnt_ref, xbuf, x_hbm, sems, tile, slot, act):
    run = 0
    for e in range(N_EXPERTS):
        cnt = cnt_ref[tile * N_EXPERTS + e]
        st = start_ref[tile * N_EXPERTS + e]

        @pl.when(cnt > 0)
        def _(run=run, cnt=cnt, st=st, e=e):
            src = xbuf.at[slot, pl.ds(pl.multiple_of(run * SUBLANES, SUBLANES), cnt * SUBLANES)]
            dst = x_hbm.at[e, pl.ds(pl.multiple_of(st * SUBLANES, SUBLANES), cnt * SUBLANES)]
            act(pltpu.make_async_copy(src, dst, sems.at[slot]), e)
        run = run + cnt


def _dispatch_kernel(start_ref, cnt_ref, pos_ref, h2_ref, x_hbm, xbuf, sems):
    i = pl.program_id(0)
    last = pl.num_programs(0) - 1
    slot = i % 2
    tt, d = h2_ref.shape

    n_e = N_EXPERTS
    ch = DISPATCH_CHUNK

    row_e = lax.broadcasted_iota(jnp.int32, (n_e, 1), 0)
    shift = jnp.zeros((n_e, 1), jnp.int32)
    ends = []
    run = 0
    for e in range(n_e):
        shift = jnp.where(row_e == e, run - start_ref[i * n_e + e], shift)
        run = run + cnt_ref[i * n_e + e]
        ends.append(run)
    n_rows = run
    pos = pos_ref[0]
    col = jnp.where(pos >= 0, pos + shift, _NO_COLUMN)
    col_hi = (col >> _COL_BITS).astype(_F32).astype(_BF16)
    col_lo = (col & (_COL_SPLIT - 1)).astype(_F32).astype(_BF16)

    for k in range(_DISPATCH_ROWS // ch):
        @pl.when(k * ch < n_rows)
        def _(k=k):
            c_i = k * ch + lax.broadcasted_iota(jnp.int32, (1, ch), 1)
            owner = jnp.zeros((1, ch), jnp.int32)
            for end in ends:
                owner = owner + jnp.where(c_i >= end, 1, 0)
            spread = jnp.where(lax.broadcasted_iota(jnp.int32, (n_e, ch), 0) == owner, 1.0, 0.0).astype(_BF16)
            down = lambda a: lax.dot_general(spread, a, (((0,), (0,)), ((), ())), preferred_element_type=_F32)
            row_c = _COL_SPLIT * down(col_hi) + down(col_lo)
            c_col = (k * ch + lax.broadcasted_iota(jnp.int32, (ch, tt), 0)).astype(_F32)
            ptb = jnp.where(row_c == c_col, 1.0, 0.0).astype(_BF16)
            for j2 in range(d // (2 * LANES)):
                r = _dot(ptb, h2_ref[:, j2 * 2 * LANES:(j2 + 1) * 2 * LANES])
                for h in range(2):
                    rows = pl.ds(k * ch * SUBLANES + 2 * j2 + h, ch, stride=SUBLANES)
                    xbuf[slot, rows, :] = r[:, h * LANES:(h + 1) * LANES]

    @pl.when(i > 0)
    def _():
        _dispatch_copies(start_ref, cnt_ref, xbuf, x_hbm, sems, i - 1, 1 - slot, _wait_copy)

    _dispatch_copies(start_ref, cnt_ref, xbuf, x_hbm, sems, i, slot, _start_copy)

    @pl.when(i == last)
    def _():
        _dispatch_copies(start_ref, cnt_ref, xbuf, x_hbm, sems, i, slot, _wait_copy)


def _dispatch(h2, pos, start_tab, cnt_tab, cap):
    n_tok, d = h2.shape
    n_e, n_rows, _ = pos.shape
    assert d == SUBLANES * LANES, "one activation row must fill one 8 x 128 tile"
    n_tiles = n_tok // DISPATCH_TILE
    return pl.pallas_call(
        _dispatch_kernel,
        out_shape=jax.ShapeDtypeStruct((n_e, cap * SUBLANES, LANES), _F32),
        grid_spec=pltpu.PrefetchScalarGridSpec(
            num_scalar_prefetch=2,
            grid=(n_tiles,),
            in_specs=[
                pl.BlockSpec((1, n_e, DISPATCH_TILE), lambda i, s, c: (i, 0, 0)),
                pl.BlockSpec((DISPATCH_TILE, d), lambda i, s, c: (i, 0)),
            ],
            out_specs=pl.BlockSpec(memory_space=pl.ANY),
            scratch_shapes=[
                pltpu.VMEM((2, _DISPATCH_ROWS * SUBLANES, LANES), _F32),
                pltpu.SemaphoreType.DMA((2,)),
            ],
        ),
        compiler_params=pltpu.CompilerParams(
            dimension_semantics=("arbitrary",), vmem_limit_bytes=VMEM_LIMIT),
        name="dispatch",
    )(start_tab, cnt_tab, pos.reshape(n_e, n_tiles, DISPATCH_TILE).transpose(1, 0, 2), h2)


def _ffn_kernel(x_ref, wg_ref, wu_ref, wd_ref, y_ref, xb):
    f = pl.program_id(2)
    n_j = SUBLANES
    mt = xb.shape[0]

    @pl.when(f == 0)
    def _():
        for j in range(n_j):
            xb[:, j * LANES:(j + 1) * LANES] = x_ref[0, pl.ds(j, mt, stride=SUBLANES), :].astype(_BF16)

    def body(first):
        wg = wg_ref[0, 0].astype(_BF16)
        wu = wu_ref[0, 0].astype(_BF16)
        wd = wd_ref[0, 0].astype(_BF16)
        rb = min(FFN_SUB_ROWS, mt)
        for s in range(mt // rb):
            x = xb[s * rb:(s + 1) * rb, :]
            g = _dot(x, wg)
            u = _dot(x, wu)
            hid = (g * _sigmoid(g) * u).astype(_BF16)
            res = _dot(hid, wd)
            for j in range(n_j):
                part = res[:, j * LANES:(j + 1) * LANES]
                if first:
                    y_ref[0, j, s * rb:(s + 1) * rb, :] = part
                else:
                    y_ref[0, j, s * rb:(s + 1) * rb, :] += part

    pl.when(f == 0)(functools.partial(body, True))
    pl.when(f > 0)(functools.partial(body, False))


def _ffn(x_exp, w_gate, w_up, w_down, layer):
    n_e = x_exp.shape[0]
    n_j = SUBLANES
    cap = x_exp.shape[1] // n_j
    d = n_j * LANES
    d_exp = w_gate.shape[-1]
    mt = min(FFN_ROWS, cap)
    fc = min(FFN_COLS, d_exp)
    return pl.pallas_call(
        _ffn_kernel,
        out_shape=jax.ShapeDtypeStruct((n_e, n_j, cap, LANES), _F32),
        grid=(n_e, cap // mt, d_exp // fc),
        in_specs=[
            pl.BlockSpec((1, mt * n_j, LANES), lambda e, m, f: (e, m, 0)),
            pl.BlockSpec((1, 1, d, fc), lambda e, m, f: (layer, e, 0, f)),
            pl.BlockSpec((1, 1, d, fc), lambda e, m, f: (layer, e, 0, f)),
            pl.BlockSpec((1, 1, fc, d), lambda e, m, f: (layer, e, f, 0)),
        ],
        out_specs=pl.BlockSpec((1, n_j, mt, LANES), lambda e, m, f: (e, 0, m, 0)),
        scratch_shapes=[pltpu.VMEM((mt, d), _BF16)],
        compiler_params=pltpu.CompilerParams(
            dimension_semantics=("arbitrary", "arbitrary", "arbitrary"), vmem_limit_bytes=VMEM_LIMIT),
        name="ffn",
    )(x_exp, w_gate, w_up, w_down)


_COMBINE_ROWS = _round_up(N_EXPERTS * (COMBINE_TILE + 2 * SUBLANES), COMBINE_CHUNK)
assert _COMBINE_ROWS <= _NO_COLUMN


def _combine_windows(start_ref, cnt_ref, tile):
    out = []
    run = 0
    for e in range(N_EXPERTS):
        cnt = cnt_ref[tile * N_EXPERTS + e]
        st = start_ref[tile * N_EXPERTS + e]
        a0 = (st >> 3) << 3
        a1 = ((st + cnt + SUBLANES - 1) >> 3) << 3
        rows = jnp.where(cnt > 0, a1 - a0, 0)
        out.append((a0, rows, run))
        run = run + rows
    return out, run


def _combine_copies(start_ref, cnt_ref, y_hbm, ybuf, sems, tile, slot, act):
    wins, _ = _combine_windows(start_ref, cnt_ref, tile)
    for e, (a0, rows, boff) in enumerate(wins):
        @pl.when(rows > 0)
        def _(a0=a0, rows=rows, boff=boff, e=e):
            n = pl.multiple_of(rows, SUBLANES)
            src = y_hbm.at[e, :, pl.ds(pl.multiple_of(a0, SUBLANES), n), :]
            dst = ybuf.at[slot, :, pl.ds(pl.multiple_of(boff, SUBLANES), n), :]
            act(pltpu.make_async_copy(src, dst, sems.at[slot]), e)


def _combine_kernel(start_ref, cnt_ref, pos_ref, aff_ref, x1_ref, gt_ref, gf_ref, y_hbm, o_ref,
                    ybuf, acc, sems, *, final_norm):
    i = pl.program_id(0)
    n_tiles = pl.num_programs(0)
    slot = i % 2
    tt, d = x1_ref.shape

    @pl.when(i == 0)
    def _():
        ybuf[...] = jnp.zeros_like(ybuf)
        _combine_copies(start_ref, cnt_ref, y_hbm, ybuf, sems, i, slot, _start_copy)

    @pl.when(i + 1 < n_tiles)
    def _():
        _combine_copies(start_ref, cnt_ref, y_hbm, ybuf, sems, i + 1, 1 - slot, _start_copy)

    _combine_copies(start_ref, cnt_ref, y_hbm, ybuf, sems, i, slot, _wait_copy)

    wins, n_rows = _combine_windows(start_ref, cnt_ref, i)
    n_e = pos_ref.shape[1]
    ch = COMBINE_CHUNK

    lane_e = lax.broadcasted_iota(jnp.int32, (1, n_e), 1)
    shift = jnp.zeros((1, n_e), jnp.int32)
    for e, (a0, _, boff) in enumerate(wins):
        shift = jnp.where(lane_e == e, boff - a0, shift)
    pos = pos_ref[...]
    col = jnp.where(pos >= 0, pos + shift, _NO_COLUMN)
    col_hi = (col >> _COL_BITS).astype(_F32).astype(_BF16)
    col_lo = (col & (_COL_SPLIT - 1)).astype(_F32).astype(_BF16)
    aff = aff_ref[...]
    a_hi = aff.astype(_BF16)
    a_lo = (aff - a_hi.astype(_F32)).astype(_BF16)

    acc[...] = jnp.zeros_like(acc)
    for k in range(_COMBINE_ROWS // ch):
        @pl.when(k * ch < n_rows)
        def _(k=k):
            c_i = k * ch + lax.broadcasted_iota(jnp.int32, (1, ch), 1)
            owner = jnp.zeros((1, ch), jnp.int32)
            for _, rows, boff in wins:
                owner = owner + jnp.where(c_i >= boff + rows, 1, 0)
            spread = jnp.where(lax.broadcasted_iota(jnp.int32, (n_e, ch), 0) == owner, 1.0, 0.0).astype(_BF16)
            col_c = _COL_SPLIT * _dot(col_hi, spread) + _dot(col_lo, spread)
            hit = col_c == c_i.astype(_F32)
            ph = jnp.where(hit, _dot(a_hi, spread), 0.0).astype(_BF16)
            plo = jnp.where(hit, _dot(a_lo, spread), 0.0).astype(_BF16)
            for j2 in range(d // (2 * LANES)):
                y = jnp.concatenate([ybuf[slot, 2 * j2, pl.ds(k * ch, ch), :],
                                     ybuf[slot, 2 * j2 + 1, pl.ds(k * ch, ch), :]], axis=1)
                yh = y.astype(_BF16)
                ylo = (y - yh.astype(_F32)).astype(_BF16)
                acc[:, j2 * 2 * LANES:(j2 + 1) * 2 * LANES] += _dot(ph, yh) + (_dot(ph, ylo) + _dot(plo, yh))

    x2 = x1_ref[...] + gt_ref[0] * acc[...]
    if final_norm:
        x2 = x2 * lax.rsqrt(jnp.mean(x2 * x2, axis=-1, keepdims=True) + EPS) * gf_ref[...]
    o_ref[...] = x2


def _combine(y_exp, pos_t, aff_tok, x1, gt2, g_final, start_tab, cnt_tab, seq_len, final_norm):
    n_tok, d = x1.shape
    n_e = y_exp.shape[0]
    n_tiles = n_tok // COMBINE_TILE
    tiles_per_seq = seq_len // COMBINE_TILE
    return pl.pallas_call(
        functools.partial(_combine_kernel, final_norm=final_norm),
        out_shape=jax.ShapeDtypeStruct((n_tok, d), _F32),
        grid_spec=pltpu.PrefetchScalarGridSpec(
            num_scalar_prefetch=2,
            grid=(n_tiles,),
            in_specs=[
                pl.BlockSpec((COMBINE_TILE, n_e), lambda i, s, c: (i, 0)),
                pl.BlockSpec((COMBINE_TILE, n_e), lambda i, s, c: (i, 0)),
                pl.BlockSpec((COMBINE_TILE, d), lambda i, s, c: (i, 0)),
                pl.BlockSpec((1, 1, d), lambda i, s, c: (i // tiles_per_seq, 0, 0)),
                pl.BlockSpec((1, d), lambda i, s, c: (0, 0)),
                pl.BlockSpec(memory_space=pl.ANY),
            ],
            out_specs=pl.BlockSpec((COMBINE_TILE, d), lambda i, s, c: (i, 0)),
            scratch_shapes=[
                pltpu.VMEM((2, d // LANES, _COMBINE_ROWS, LANES), _F32),
                pltpu.VMEM((COMBINE_TILE, d), _F32),
                pltpu.SemaphoreType.DMA((2,)),
            ],
        ),
        compiler_params=pltpu.CompilerParams(
            dimension_semantics=("arbitrary",), vmem_limit_bytes=VMEM_LIMIT),
        name="combine",
    )(start_tab, cnt_tab, pos_t, aff_tok, x1, gt2, g_final, y_exp)


def _moe(x1, h2, aff_tok, gt2, g_final, w_gate, w_up, w_down, layer, final_norm):
    bsz, s, d = x1.shape
    n_tok, n_e = aff_tok.shape
    cap = max(1, CAPACITY_FACTOR * n_tok // n_e)
    pos, start_rep = _route(aff_tok.T, cap)
    row_start = start_rep[:, :, 0]

    def tile_tables(tile):
        first = row_start[:, ::tile // LANES]
        end = jnp.concatenate([first[:, 1:], jnp.full((n_e, 1), cap, jnp.int32)], axis=1)
        return first.T.reshape(-1), (end - first).T.reshape(-1)

    x_exp = _dispatch(h2.reshape(n_tok, d), pos, *tile_tables(DISPATCH_TILE), cap)
    y_exp = _ffn(x_exp, w_gate, w_up, w_down, layer)
    x2 = _combine(y_exp, pos.reshape(n_e, n_tok).T, aff_tok, x1.reshape(n_tok, d), gt2, g_final,
                  *tile_tables(COMBINE_TILE), s, final_norm)
    return x2.reshape(bsz, s, d)


def kernel(x_prompt, x_sample, c_prompt, c_sample, w_ada, b_ada, g_norm1, g_norm2, w_in, b_in, conv_w, conv_b, conv_ln_g, conv_ln_b, sgu_ln_g, sgu_ln_b, sgu_w, sgu_b, pool_w, pool_scale, w_branch, w_out, w_router, w_e_gate, w_e_up, w_e_down, g_final):
    depth = w_ada.shape[0]
    d = x_prompt.shape[-1]
    groups = ((x_prompt, c_prompt), (x_sample, c_sample))

    n_c = sum(c.shape[0] for _, c in groups)
    pad = -n_c % SUBLANES
    c_all = jnp.concatenate([c for _, c in groups] + [jnp.zeros((pad, d), _F32)], axis=0)
    mod_all = _adaln(c_all, w_ada, b_ada)

    row = lambda v: v.reshape(1, -1)
    layers = []
    for l in range(depth):
        wr_hi = w_router[l].astype(_BF16)
        layers.append(dict(
            g1=row(g_norm1[l]), g2=row(g_norm2[l]),
            w_in=w_in[l].astype(_BF16), b_in=row(b_in[l]),
            conv_w=jnp.broadcast_to(conv_w[l][:, None, :], (CONV_WIDTH, SUBLANES, d)), conv_b=row(conv_b[l]),
            conv_ln_g=row(conv_ln_g[l]), conv_ln_b=row(conv_ln_b[l]),
            sgu_ln_g=row(sgu_ln_g[l]), sgu_ln_b=row(sgu_ln_b[l]),
            sgu_w=sgu_w[l].astype(_BF16),
            sgu_b=jnp.repeat(sgu_b[l].T, d // SGU_HEADS, axis=1),
            pool_w=pool_w[l].astype(_BF16), pool_scale=row(pool_scale[l]),
            w_branch=w_branch[l].astype(_BF16), w_out=w_out[l].astype(_BF16),
            w_router_hi=wr_hi, w_router_lo=(w_router[l] - wr_hi.astype(_F32)).astype(_BF16),
        ))
    gf = row(g_final)

    outs = []
    c0 = 0
    for x, c in groups:
        bsz = x.shape[0]
        for l in range(depth):
            mod = mod_all[l, c0:c0 + bsz].reshape(bsz, 6, d)
            x1, h2, aff_tok = _mixer(x, mod, layers[l])
            x = _moe(x1, h2, aff_tok, mod[:, 5:6, :], gf, w_e_gate, w_e_up, w_e_down, l,
                     final_norm=(l == depth - 1))
        outs.append(x)
        c0 += bsz
    return tuple(outs)
```

```python
import functools

import jax
import jax.numpy as jnp
from jax import lax
from jax.experimental import pallas as pl
from jax.experimental.pallas import tpu as pltpu

EPS = 1e-6
CONV_WIDTH = 31
CONV_PAD = CONV_WIDTH // 2
SGU_CHUNK = 128
SGU_HEADS = 8
POOL_WINDOWS = (2, 4, 8, 16)
N_BRANCH = 3
N_EXPERTS = 16
CAPACITY_FACTOR = 2

LANES = 128
SUBLANES = 8
HALO = 16
MIX_TILE = 256
DISPATCH_TILE = 256
COMBINE_TILE = 256
DISPATCH_CHUNK = 640
COMBINE_CHUNK = 768
FFN_ROWS = 1024
FFN_COLS = 1024
FFN_SUB_ROWS = 512
VMEM_LIMIT = 56 * 1024 * 1024
NOT_SELECTED = -(1 << 20)

_F32 = jnp.float32
_BF16 = jnp.bfloat16


def _dot(a, b):
    return jnp.dot(a, b, preferred_element_type=_F32)


def _sigmoid(x):
    return 1.0 / (1.0 + jnp.exp(-x))


def _rms_mod(x, scale, shift):
    return x * lax.rsqrt(jnp.mean(x * x, axis=-1, keepdims=True) + EPS) * scale + shift


def _layernorm(x, g, b):
    mu = jnp.mean(x, axis=-1, keepdims=True)
    xc = x - mu
    var = jnp.mean(xc * xc, axis=-1, keepdims=True)
    return xc * lax.rsqrt(var + EPS) * g + b


def _adaln_kernel(c_ref, w_ref, b_ref, o_ref):
    c = c_ref[...]
    s = c * _sigmoid(c)
    o_ref[0] = jnp.dot(s, w_ref[0], preferred_element_type=_F32,
                       precision=lax.Precision.HIGHEST) + b_ref[0]


def _adaln(c_all, w_ada, b_ada):
    depth, d, n = w_ada.shape
    rows = c_all.shape[0]
    return pl.pallas_call(
        _adaln_kernel,
        out_shape=jax.ShapeDtypeStruct((depth, rows, n), _F32),
        grid=(depth, n // d),
        in_specs=[
            pl.BlockSpec((rows, d), lambda l, j: (0, 0)),
            pl.BlockSpec((1, d, d), lambda l, j: (l, 0, j)),
            pl.BlockSpec((1, 1, d), lambda l, j: (l, 0, j)),
        ],
        out_specs=pl.BlockSpec((1, rows, d), lambda l, j: (l, 0, j)),
        name="adaln",
    )(c_all, w_ada, b_ada.reshape(depth, 1, n))


def _window_sum(x, w):
    n = x.shape[0]
    a = x
    span = 1
    while span < w // 2:
        a = a + pltpu.roll(a, n - span, axis=0)
        span *= 2
    return pltpu.roll(a, w // 2, axis=0) + a


def _mixer_kernel(xp_ref, x_ref, xn_ref, mod_ref, g1_ref, g2_ref, win_ref, bin_ref,
                  convw_ref, convb_ref, clng_ref, clnb_ref, slng_ref, slnb_ref, sguw_ref, sgub_ref,
                  poolw_ref, pscale_ref, wbr_ref, wout_ref, wrh_ref, wrl_ref,
                  x1_ref, h2_ref, aff_ref,
                  hbuf, gbuf, sbuf, cbuf, ybuf, *, seq_len):
    tm = x_ref.shape[1]
    d = x_ref.shape[2]
    i = pl.program_id(1)
    n_tiles = pl.num_programs(1)

    sh1 = mod_ref[0, 0:1, :]
    sc1 = mod_ref[0, 1:2, :]
    gt1 = mod_ref[0, 2:3, :]
    sh2 = mod_ref[0, 3:4, :]
    sc2 = mod_ref[0, 4:5, :]
    scale1 = g1_ref[...] * (1.0 + sc1)
    scale2 = g2_ref[...] * (1.0 + sc2)

    xm = x_ref[0]
    hbuf[0:HALO, :] = _rms_mod(xp_ref[0], scale1, sh1).astype(_BF16)
    hbuf[HALO:HALO + tm, :] = _rms_mod(xm, scale1, sh1).astype(_BF16)
    hbuf[HALO + tm:, :] = _rms_mod(xn_ref[0], scale1, sh1).astype(_BF16)

    rows = lax.broadcasted_iota(jnp.int32, (tm + 2 * HALO, 1), 0)
    valid = jnp.logical_and(jnp.logical_or(rows >= HALO, i > 0),
                            jnp.logical_or(rows < tm + HALO, i < n_tiles - 1))

    o_u = 2 * d
    o_v = 3 * d
    o_c = 4 * d
    o_g = 5 * d

    nt_cols = 256
    for n0 in range(0, d, nt_cols):
        a = _dot(hbuf[...], win_ref[:, n0:n0 + nt_cols]) + bin_ref[:, n0:n0 + nt_cols]
        gt = _dot(hbuf[...], win_ref[:, d + n0:d + n0 + nt_cols]) + bin_ref[:, d + n0:d + n0 + nt_cols]
        gbuf[:, n0:n0 + nt_cols] = jnp.where(valid, a * _sigmoid(gt), 0.0)
    n_sh = sbuf.shape[1]
    for r in range(1, SUBLANES):
        sbuf[r - 1] = gbuf[r:r + n_sh, :]
    n_grp = 4
    rc = n_grp * SUBLANES
    for c in range(tm // rc):
        r0 = c * rc
        accs = [None] * n_grp
        for k in range(CONV_WIDTH):
            q, r = divmod(HALO - CONV_PAD + k, SUBLANES)
            w8 = convw_ref[k]
            for g in range(n_grp):
                lo = r0 + (q + g) * SUBLANES
                src = gbuf[lo:lo + SUBLANES, :] if r == 0 else sbuf[r - 1, lo:lo + SUBLANES, :]
                term = src * w8
                accs[g] = term if accs[g] is None else accs[g] + term
        acc = jnp.concatenate(accs, axis=0)
        y = _layernorm(acc + convb_ref[...], clng_ref[...], clnb_ref[...])
        ybuf[0, r0:r0 + rc, :] = (y * _sigmoid(y)).astype(_BF16)

    hm = hbuf[HALO:HALO + tm, :]
    pu = _dot(hm, win_ref[:, o_u:o_v]) + bin_ref[:, o_u:o_v]
    pv = _dot(hm, win_ref[:, o_v:o_c]) + bin_ref[:, o_v:o_c]
    vn = _layernorm(pv, slng_ref[...], slnb_ref[...]).astype(_BF16)
    dh = d // SGU_HEADS
    lane2 = lax.broadcasted_iota(jnp.int32, (SGU_CHUNK, 2 * dh), 1)
    zero = jnp.zeros((SGU_CHUNK, 2 * dh), _BF16)
    for n in range(tm // SGU_CHUNK):
        r0 = n * SGU_CHUNK
        for hp in range(SGU_HEADS // 2):
            c0 = 2 * hp * dh
            v2 = vn[r0:r0 + SGU_CHUNK, c0:c0 + 2 * dh]
            vdiag = jnp.concatenate([jnp.where(lane2 < dh, v2, zero), jnp.where(lane2 >= dh, v2, zero)], axis=0)
            vm = _dot(sguw_ref[hp], vdiag) + sgub_ref[:, c0:c0 + 2 * dh]
            ybuf[1, r0:r0 + SGU_CHUNK, c0:c0 + 2 * dh] = (pu[r0:r0 + SGU_CHUNK, c0:c0 + 2 * dh] * vm).astype(_BF16)

    pc = _dot(hbuf[...], win_ref[:, o_c:o_g]) + bin_ref[:, o_c:o_g]
    cbuf[...] = jnp.where(valid, pc, 0.0)
    pos = i * tm + lax.broadcasted_iota(jnp.int32, (tm, 1), 0)
    gc = d // len(POOL_WINDOWS)
    for g, w in enumerate(POOL_WINDOWS):
        c0 = g * gc
        half = w // 2
        tot = _window_sum(cbuf[:, c0:c0 + gc], w)[HALO:HALO + tm]
        lo = jnp.maximum(pos - half, 0)
        hi = jnp.minimum(pos - half + w, seq_len)
        mean = tot / (hi - lo).astype(_F32)
        diff = mean - cbuf[HALO:HALO + tm, c0:c0 + gc]
        yc = _dot(diff.astype(_BF16), poolw_ref[g]) * pscale_ref[:, c0:c0 + gc]
        ybuf[2, :, c0:c0 + gc] = yc.astype(_BF16)

    parts = []
    for n0 in range(0, d, nt_cols):
        part = None
        for b in range(N_BRANCH):
            lo = o_g + b * d + n0
            gate = _sigmoid(_dot(hm, win_ref[:, lo:lo + nt_cols]) + bin_ref[:, lo:lo + nt_cols])
            term = _dot(ybuf[b], wbr_ref[b, :, n0:n0 + nt_cols]) * gate
            part = term if part is None else part + term
        parts.append(part.astype(_BF16))
    mix = _dot(jnp.concatenate(parts, axis=1), wout_ref[...])
    x1 = xm + gt1 * mix
    x1_ref[0] = x1

    h2 = _rms_mod(x1, scale2, sh2)
    h2_hi = h2.astype(_BF16)
    h2_ref[0] = h2_hi
    h2_lo = (h2 - h2_hi.astype(_F32)).astype(_BF16)
    logits = _dot(h2_hi, wrh_ref[...]) + (_dot(h2_lo, wrh_ref[...]) + _dot(h2_hi, wrl_ref[...]))
    m = jnp.max(logits, axis=1, keepdims=True)
    ex = jnp.exp(logits - m)
    aff_ref[...] = ex / jnp.sum(ex, axis=1, keepdims=True)


def _const_spec(shape):
    nd = len(shape)
    return pl.BlockSpec(shape, lambda b, i: (0,) * nd, pipeline_mode=pl.Buffered(1))


def _mixer(x, mod, lw, tm=MIX_TILE):
    bsz, s, d = x.shape
    nt = s // tm
    hb = tm // HALO
    n_in = lw["w_in"].shape[1]
    x_halo = lambda f: pl.BlockSpec((1, HALO, d), f)
    return pl.pallas_call(
        functools.partial(_mixer_kernel, seq_len=s),
        out_shape=(
            jax.ShapeDtypeStruct((bsz, s, d), _F32),
            jax.ShapeDtypeStruct((bsz, s, d), _BF16),
            jax.ShapeDtypeStruct((bsz * s, N_EXPERTS), _F32),
        ),
        grid=(bsz, nt),
        in_specs=[
            x_halo(lambda b, i: (b, jnp.maximum(i * hb - 1, 0), 0)),
            pl.BlockSpec((1, tm, d), lambda b, i: (b, i, 0)),
            x_halo(lambda b, i: (b, jnp.minimum((i + 1) * hb, s // HALO - 1), 0)),
            pl.BlockSpec((1, 6, d), lambda b, i: (b, 0, 0)),
            _const_spec((1, d)), _const_spec((1, d)),
            _const_spec((d, n_in)), _const_spec((1, n_in)),
            _const_spec((CONV_WIDTH, SUBLANES, d)), _const_spec((1, d)), _const_spec((1, d)), _const_spec((1, d)),
            _const_spec((1, d)), _const_spec((1, d)),
            _const_spec((SGU_HEADS // 2, SGU_CHUNK, 2 * SGU_CHUNK)), _const_spec((SGU_CHUNK, d)),
            _const_spec(lw["pool_w"].shape), _const_spec((1, d)),
            _const_spec((N_BRANCH, d, d)), _const_spec((d, d)),
            _const_spec((d, N_EXPERTS)), _const_spec((d, N_EXPERTS)),
        ],
        out_specs=(
            pl.BlockSpec((1, tm, d), lambda b, i: (b, i, 0)),
            pl.BlockSpec((1, tm, d), lambda b, i: (b, i, 0)),
            pl.BlockSpec((tm, N_EXPERTS), lambda b, i: (b * nt + i, 0)),
        ),
        scratch_shapes=[
            pltpu.VMEM((tm + 2 * HALO, d), _BF16),
            pltpu.VMEM((tm + 2 * HALO, d), _F32),
            pltpu.VMEM((SUBLANES - 1, tm + 2 * HALO - SUBLANES, d), _F32),
            pltpu.VMEM((tm + 2 * HALO, d), _F32),
            pltpu.VMEM((N_BRANCH, tm, d), _BF16),
        ],
        compiler_params=pltpu.CompilerParams(
            dimension_semantics=("arbitrary", "arbitrary"), vmem_limit_bytes=VMEM_LIMIT),
        name="mixer",
    )(x, x, x, mod, lw["g1"], lw["g2"], lw["w_in"], lw["b_in"], lw["conv_w"], lw["conv_b"],
      lw["conv_ln_g"], lw["conv_ln_b"], lw["sgu_ln_g"], lw["sgu_ln_b"], lw["sgu_w"], lw["sgu_b"],
      lw["pool_w"], lw["pool_scale"], lw["w_branch"], lw["w_out"], lw["w_router_hi"], lw["w_router_lo"])


def _route_kernel(aff_ref, pos_ref, start_ref, *, cap, n_tok):
    n_e, n_rows, _ = aff_ref.shape
    keys = pltpu.bitcast(aff_ref[...], jnp.int32)

    def count(ind):
        return jnp.sum(jnp.sum(ind, axis=1, keepdims=True), axis=2, keepdims=True)

    def thr_step(it, prefix):
        cand = prefix | jnp.left_shift(jnp.int32(1), 30 - it)
        c = count(jnp.where(keys >= cand, 1.0, 0.0))
        return jnp.where(c >= cap, cand, prefix)

    thr = lax.fori_loop(0, 31, thr_step, jnp.zeros((n_e, 1, 1), jnp.int32))
    gt = keys > thr
    eq = keys == thr
    need = cap - count(jnp.where(gt, 1.0, 0.0))
    tidx = (lax.broadcasted_iota(jnp.int32, (1, n_rows, LANES), 1) * LANES
            + lax.broadcasted_iota(jnp.int32, (1, n_rows, LANES), 2))
    n_bits = max(1, (n_tok - 1).bit_length())

    def tie_step(it, bound):
        cand = bound + jnp.left_shift(jnp.int32(1), n_bits - 1 - it)
        c = count(jnp.where(eq, jnp.where(tidx < cand, 1.0, 0.0), 0.0))
        return jnp.where(c < need, cand, bound)

    bound = lax.fori_loop(0, n_bits, tie_step, jnp.zeros((n_e, 1, 1), jnp.int32))
    sel = jnp.where(gt, 1.0, jnp.where(eq, jnp.where(tidx <= bound, 1.0, 0.0), 0.0))

    r_i = lax.broadcasted_iota(jnp.int32, (LANES, LANES), 0)
    c_i = lax.broadcasted_iota(jnp.int32, (LANES, LANES), 1)
    upper = jnp.where(r_i <= c_i, 1.0, 0.0).astype(_BF16)
    ones = jnp.ones((LANES, LANES), _BF16)
    rr = lax.broadcasted_iota(jnp.int32, (n_rows, n_rows), 0)
    rc = lax.broadcasted_iota(jnp.int32, (n_rows, n_rows), 1)
    lower = jnp.where(rc < rr, 1.0, 0.0).astype(_BF16)
    for e in range(n_e):
        s = sel[e]
        sb = s.astype(_BF16)
        incl = _dot(sb, upper)
        tot = _dot(sb, ones)
        start = _dot(lower, tot.astype(_BF16))
        slot = (start + incl).astype(jnp.int32) - 1
        pos_ref[e] = jnp.where(s > 0.5, slot, NOT_SELECTED)
        start_ref[e] = start.astype(jnp.int32)


def _route(aff_t, cap):
    n_e, n_tok = aff_t.shape
    n_rows = n_tok // LANES
    return pl.pallas_call(
        functools.partial(_route_kernel, cap=cap, n_tok=n_tok),
        out_shape=(
            jax.ShapeDtypeStruct((n_e, n_rows, LANES), jnp.int32),
            jax.ShapeDtypeStruct((n_e, n_rows, LANES), jnp.int32),
        ),
        name="route",
    )(aff_t.reshape(n_e, n_rows, LANES))


def _round_up(n, m):
    return -(-n // m) * m


_DISPATCH_ROWS = _round_up(N_EXPERTS * DISPATCH_TILE, DISPATCH_CHUNK)
_COL_BITS = 7
_COL_SPLIT = 1 << _COL_BITS
_NO_COLUMN = _COL_SPLIT * _COL_SPLIT - 1
assert _DISPATCH_ROWS <= _NO_COLUMN


def _dispatch_copies(start_ref, cnt_ref, xbuf, x_hbm, sems, tile, slot, act):
    run = 0
    for e in range(N_EXPERTS):
        cnt = cnt_ref[tile * N_EXPERTS + e]
        st = start_ref[tile * N_EXPERTS + e]

        @pl.when(cnt > 0)
        def _(run=run, cnt=cnt, st=st, e=e):
            src = xbuf.at[slot, pl.ds(pl.multiple_of(run * SUBLANES, SUBLANES), cnt * SUBLANES)]
            dst = x_hbm.at[e, pl.ds(pl.multiple_of(st * SUBLANES, SUBLANES), cnt * SUBLANES)]
            act(pltpu.make_async_copy(src, dst, sems.at[slot]))
        run = run + cnt


def _dispatch_kernel(start_ref, cnt_ref, pos_ref, h2_ref, x_hbm, xbuf, sems):
    i = pl.program_id(0)
    last = pl.num_programs(0) - 1
    slot = i % 2
    tt, d = h2_ref.shape

    n_e = N_EXPERTS
    ch = DISPATCH_CHUNK

    row_e = lax.broadcasted_iota(jnp.int32, (n_e, 1), 0)
    shift = jnp.zeros((n_e, 1), jnp.int32)
    ends = []
    run = 0
    for e in range(n_e):
        shift = jnp.where(row_e == e, run - start_ref[i * n_e + e], shift)
        run = run + cnt_ref[i * n_e + e]
        ends.append(run)
    n_rows = run
    pos = pos_ref[0]
    col = jnp.where(pos >= 0, pos + shift, _NO_COLUMN)
    col_hi = (col >> _COL_BITS).astype(_F32).astype(_BF16)
    col_lo = (col & (_COL_SPLIT - 1)).astype(_F32).astype(_BF16)

    for k in range(_DISPATCH_ROWS // ch):
        @pl.when(k * ch < n_rows)
        def _(k=k):
            c_i = k * ch + lax.broadcasted_iota(jnp.int32, (1, ch), 1)
            owner = jnp.zeros((1, ch), jnp.int32)
            for end in ends:
                owner = owner + jnp.where(c_i >= end, 1, 0)
            spread = jnp.where(lax.broadcasted_iota(jnp.int32, (n_e, ch), 0) == owner, 1.0, 0.0).astype(_BF16)
            down = lambda a: lax.dot_general(spread, a, (((0,), (0,)), ((), ())), preferred_element_type=_F32)
            row_c = _COL_SPLIT * down(col_hi) + down(col_lo)
            c_col = (k * ch + lax.broadcasted_iota(jnp.int32, (ch, tt), 0)).astype(_F32)
            ptb = jnp.where(row_c == c_col, 1.0, 0.0).astype(_BF16)
            for j2 in range(d // (2 * LANES)):
                r = _dot(ptb, h2_ref[:, j2 * 2 * LANES:(j2 + 1) * 2 * LANES])
                for h in range(2):
                    rows = pl.ds(k * ch * SUBLANES + 2 * j2 + h, ch, stride=SUBLANES)
                    xbuf[slot, rows, :] = r[:, h * LANES:(h + 1) * LANES]

    @pl.when(i > 0)
    def _():
        _dispatch_copies(start_ref, cnt_ref, xbuf, x_hbm, sems, i - 1, 1 - slot, lambda cp: cp.wait())

    _dispatch_copies(start_ref, cnt_ref, xbuf, x_hbm, sems, i, slot, lambda cp: cp.start())

    @pl.when(i == last)
    def _():
        _dispatch_copies(start_ref, cnt_ref, xbuf, x_hbm, sems, i, slot, lambda cp: cp.wait())


def _dispatch(h2, pos, start_tab, cnt_tab, cap):
    n_tok, d = h2.shape
    n_e, n_rows, _ = pos.shape
    assert d == SUBLANES * LANES, "one activation row must fill one 8 x 128 tile"
    n_tiles = n_tok // DISPATCH_TILE
    return pl.pallas_call(
        _dispatch_kernel,
        out_shape=jax.ShapeDtypeStruct((n_e, cap * SUBLANES, LANES), _F32),
        grid_spec=pltpu.PrefetchScalarGridSpec(
            num_scalar_prefetch=2,
            grid=(n_tiles,),
            in_specs=[
                pl.BlockSpec((1, n_e, DISPATCH_TILE), lambda i, s, c: (i, 0, 0)),
                pl.BlockSpec((DISPATCH_TILE, d), lambda i, s, c: (i, 0)),
            ],
            out_specs=pl.BlockSpec(memory_space=pl.ANY),
            scratch_shapes=[
                pltpu.VMEM((2, _DISPATCH_ROWS * SUBLANES, LANES), _F32),
                pltpu.SemaphoreType.DMA((2,)),
            ],
        ),
        compiler_params=pltpu.CompilerParams(
            dimension_semantics=("arbitrary",), vmem_limit_bytes=VMEM_LIMIT),
        name="dispatch",
    )(start_tab, cnt_tab, pos.reshape(n_e, n_tiles, DISPATCH_TILE).transpose(1, 0, 2), h2)


def _ffn_kernel(x_ref, wg_ref, wu_ref, wd_ref, y_ref, xb):
    f = pl.program_id(2)
    n_j = SUBLANES
    mt = xb.shape[0]

    @pl.when(f == 0)
    def _():
        for j in range(n_j):
            xb[:, j * LANES:(j + 1) * LANES] = x_ref[0, pl.ds(j, mt, stride=SUBLANES), :].astype(_BF16)

    def body(first):
        wg = wg_ref[0, 0].astype(_BF16)
        wu = wu_ref[0, 0].astype(_BF16)
        wd = wd_ref[0, 0].astype(_BF16)
        rb = min(FFN_SUB_ROWS, mt)
        for s in range(mt // rb):
            x = xb[s * rb:(s + 1) * rb, :]
            g = _dot(x, wg)
            u = _dot(x, wu)
            hid = (g * _sigmoid(g) * u).astype(_BF16)
            res = _dot(hid, wd)
            for j in range(n_j):
                part = res[:, j * LANES:(j + 1) * LANES]
                if first:
                    y_ref[0, j, s * rb:(s + 1) * rb, :] = part
                else:
                    y_ref[0, j, s * rb:(s + 1) * rb, :] += part

    pl.when(f == 0)(functools.partial(body, True))
    pl.when(f > 0)(functools.partial(body, False))


def _ffn(x_exp, w_gate, w_up, w_down, layer):
    n_e = x_exp.shape[0]
    n_j = SUBLANES
    cap = x_exp.shape[1] // n_j
    d = n_j * LANES
    d_exp = w_gate.shape[-1]
    mt = min(FFN_ROWS, cap)
    fc = min(FFN_COLS, d_exp)
    return pl.pallas_call(
        _ffn_kernel,
        out_shape=jax.ShapeDtypeStruct((n_e, n_j, cap, LANES), _F32),
        grid=(n_e, cap // mt, d_exp // fc),
        in_specs=[
            pl.BlockSpec((1, mt * n_j, LANES), lambda e, m, f: (e, m, 0)),
            pl.BlockSpec((1, 1, d, fc), lambda e, m, f: (layer, e, 0, f)),
            pl.BlockSpec((1, 1, d, fc), lambda e, m, f: (layer, e, 0, f)),
            pl.BlockSpec((1, 1, fc, d), lambda e, m, f: (layer, e, f, 0)),
        ],
        out_specs=pl.BlockSpec((1, n_j, mt, LANES), lambda e, m, f: (e, 0, m, 0)),
        scratch_shapes=[pltpu.VMEM((mt, d), _BF16)],
        compiler_params=pltpu.CompilerParams(
            dimension_semantics=("arbitrary", "arbitrary", "arbitrary"), vmem_limit_bytes=VMEM_LIMIT),
        name="ffn",
    )(x_exp, w_gate, w_up, w_down)


_COMBINE_ROWS = _round_up(N_EXPERTS * (COMBINE_TILE + 2 * SUBLANES), COMBINE_CHUNK)
assert _COMBINE_ROWS <= _NO_COLUMN


def _combine_windows(start_ref, cnt_ref, tile):
    out = []
    run = 0
    for e in range(N_EXPERTS):
        cnt = cnt_ref[tile * N_EXPERTS + e]
        st = start_ref[tile * N_EXPERTS + e]
        a0 = (st >> 3) << 3
        a1 = ((st + cnt + SUBLANES - 1) >> 3) << 3
        rows = jnp.where(cnt > 0, a1 - a0, 0)
        out.append((a0, rows, run))
        run = run + rows
    return out, run


def _combine_copies(start_ref, cnt_ref, y_hbm, ybuf, sems, tile, slot, act):
    wins, _ = _combine_windows(start_ref, cnt_ref, tile)
    for e, (a0, rows, boff) in enumerate(wins):
        @pl.when(rows > 0)
        def _(a0=a0, rows=rows, boff=boff, e=e):
            n = pl.multiple_of(rows, SUBLANES)
            src = y_hbm.at[e, :, pl.ds(pl.multiple_of(a0, SUBLANES), n), :]
            dst = ybuf.at[slot, :, pl.ds(pl.multiple_of(boff, SUBLANES), n), :]
            act(pltpu.make_async_copy(src, dst, sems.at[slot]))


def _combine_kernel(start_ref, cnt_ref, pos_ref, aff_ref, x1_ref, gt_ref, gf_ref, y_hbm, o_ref,
                    ybuf, acc, sems, *, final_norm):
    i = pl.program_id(0)
    n_tiles = pl.num_programs(0)
    slot = i % 2
    tt, d = x1_ref.shape

    @pl.when(i == 0)
    def _():
        ybuf[...] = jnp.zeros_like(ybuf)
        _combine_copies(start_ref, cnt_ref, y_hbm, ybuf, sems, i, slot, lambda cp: cp.start())

    @pl.when(i + 1 < n_tiles)
    def _():
        _combine_copies(start_ref, cnt_ref, y_hbm, ybuf, sems, i + 1, 1 - slot, lambda cp: cp.start())

    _combine_copies(start_ref, cnt_ref, y_hbm, ybuf, sems, i, slot, lambda cp: cp.wait())

    wins, n_rows = _combine_windows(start_ref, cnt_ref, i)
    n_e = pos_ref.shape[1]
    ch = COMBINE_CHUNK

    lane_e = lax.broadcasted_iota(jnp.int32, (1, n_e), 1)
    shift = jnp.zeros((1, n_e), jnp.int32)
    for e, (a0, _, boff) in enumerate(wins):
        shift = jnp.where(lane_e == e, boff - a0, shift)
    pos = pos_ref[...]
    col = jnp.where(pos >= 0, pos + shift, _NO_COLUMN)
    col_hi = (col >> _COL_BITS).astype(_F32).astype(_BF16)
    col_lo = (col & (_COL_SPLIT - 1)).astype(_F32).astype(_BF16)
    aff = aff_ref[...]
    a_hi = aff.astype(_BF16)
    a_lo = (aff - a_hi.astype(_F32)).astype(_BF16)

    acc[...] = jnp.zeros_like(acc)
    for k in range(_COMBINE_ROWS // ch):
        @pl.when(k * ch < n_rows)
        def _(k=k):
            c_i = k * ch + lax.broadcasted_iota(jnp.int32, (1, ch), 1)
            owner = jnp.zeros((1, ch), jnp.int32)
            for _, rows, boff in wins:
                owner = owner + jnp.where(c_i >= boff + rows, 1, 0)
            spread = jnp.where(lax.broadcasted_iota(jnp.int32, (n_e, ch), 0) == owner, 1.0, 0.0).astype(_BF16)
            col_c = _COL_SPLIT * _dot(col_hi, spread) + _dot(col_lo, spread)
            hit = col_c == c_i.astype(_F32)
            ph = jnp.where(hit, _dot(a_hi, spread), 0.0).astype(_BF16)
            plo = jnp.where(hit, _dot(a_lo, spread), 0.0).astype(_BF16)
            for j2 in range(d // (2 * LANES)):
                y = jnp.concatenate([ybuf[slot, 2 * j2, pl.ds(k * ch, ch), :],
                                     ybuf[slot, 2 * j2 + 1, pl.ds(k * ch, ch), :]], axis=1)
                yh = y.astype(_BF16)
                ylo = (y - yh.astype(_F32)).astype(_BF16)
                acc[:, j2 * 2 * LANES:(j2 + 1) * 2 * LANES] += _dot(ph, yh) + (_dot(ph, ylo) + _dot(plo, yh))

    x2 = x1_ref[...] + gt_ref[0] * acc[...]
    if final_norm:
        x2 = x2 * lax.rsqrt(jnp.mean(x2 * x2, axis=-1, keepdims=True) + EPS) * gf_ref[...]
    o_ref[...] = x2


def _combine(y_exp, pos_t, aff_tok, x1, gt2, g_final, start_tab, cnt_tab, seq_len, final_norm):
    n_tok, d = x1.shape
    n_e = y_exp.shape[0]
    n_tiles = n_tok // COMBINE_TILE
    tiles_per_seq = seq_len // COMBINE_TILE
    return pl.pallas_call(
        functools.partial(_combine_kernel, final_norm=final_norm),
        out_shape=jax.ShapeDtypeStruct((n_tok, d), _F32),
        grid_spec=pltpu.PrefetchScalarGridSpec(
            num_scalar_prefetch=2,
            grid=(n_tiles,),
            in_specs=[
                pl.BlockSpec((COMBINE_TILE, n_e), lambda i, s, c: (i, 0)),
                pl.BlockSpec((COMBINE_TILE, n_e), lambda i, s, c: (i, 0)),
                pl.BlockSpec((COMBINE_TILE, d), lambda i, s, c: (i, 0)),
                pl.BlockSpec((1, 1, d), lambda i, s, c: (i // tiles_per_seq, 0, 0)),
                pl.BlockSpec((1, d), lambda i, s, c: (0, 0)),
                pl.BlockSpec(memory_space=pl.ANY),
            ],
            out_specs=pl.BlockSpec((COMBINE_TILE, d), lambda i, s, c: (i, 0)),
            scratch_shapes=[
                pltpu.VMEM((2, d // LANES, _COMBINE_ROWS, LANES), _F32),
                pltpu.VMEM((COMBINE_TILE, d), _F32),
                pltpu.SemaphoreType.DMA((2,)),
            ],
        ),
        compiler_params=pltpu.CompilerParams(
            dimension_semantics=("arbitrary",), vmem_limit_bytes=VMEM_LIMIT),
        name="combine",
    )(start_tab, cnt_tab, pos_t, aff_tok, x1, gt2, g_final, y_exp)


def _moe(x1, h2, aff_tok, gt2, g_final, w_gate, w_up, w_down, layer, final_norm):
    bsz, s, d = x1.shape
    n_tok, n_e = aff_tok.shape
    cap = max(1, CAPACITY_FACTOR * n_tok // n_e)
    pos, start_rep = _route(aff_tok.T, cap)
    row_start = start_rep[:, :, 0]

    def tile_tables(tile):
        first = row_start[:, ::tile // LANES]
        end = jnp.concatenate([first[:, 1:], jnp.full((n_e, 1), cap, jnp.int32)], axis=1)
        return first.T.reshape(-1), (end - first).T.reshape(-1)

    x_exp = _dispatch(h2.reshape(n_tok, d), pos, *tile_tables(DISPATCH_TILE), cap)
    y_exp = _ffn(x_exp, w_gate, w_up, w_down, layer)
    x2 = _combine(y_exp, pos.reshape(n_e, n_tok).T, aff_tok, x1.reshape(n_tok, d), gt2, g_final,
                  *tile_tables(COMBINE_TILE), s, final_norm)
    return x2.reshape(bsz, s, d)


def kernel(x_prompt, x_sample, c_prompt, c_sample, w_ada, b_ada, g_norm1, g_norm2, w_in, b_in, conv_w, conv_b, conv_ln_g, conv_ln_b, sgu_ln_g, sgu_ln_b, sgu_w, sgu_b, pool_w, pool_scale, w_branch, w_out, w_router, w_e_gate, w_e_up, w_e_down, g_final):
    depth = w_ada.shape[0]
    d = x_prompt.shape[-1]
    groups = ((x_prompt, c_prompt), (x_sample, c_sample))

    n_c = sum(c.shape[0] for _, c in groups)
    pad = -n_c % SUBLANES
    c_all = jnp.concatenate([c for _, c in groups] + [jnp.zeros((pad, d), _F32)], axis=0)
    mod_all = _adaln(c_all, w_ada, b_ada)

    row = lambda v: v.reshape(1, -1)
    layers = []
    for l in range(depth):
        wr_hi = w_router[l].astype(_BF16)
        layers.append(dict(
            g1=row(g_norm1[l]), g2=row(g_norm2[l]),
            w_in=w_in[l].astype(_BF16), b_in=row(b_in[l]),
            conv_w=jnp.broadcast_to(conv_w[l][:, None, :], (CONV_WIDTH, SUBLANES, d)), conv_b=row(conv_b[l]),
            conv_ln_g=row(conv_ln_g[l]), conv_ln_b=row(conv_ln_b[l]),
            sgu_ln_g=row(sgu_ln_g[l]), sgu_ln_b=row(sgu_ln_b[l]),
            sgu_w=sgu_w[l].astype(_BF16).reshape(SGU_HEADS // 2, 2, SGU_CHUNK, SGU_CHUNK)
            .transpose(0, 2, 1, 3).reshape(SGU_HEADS // 2, SGU_CHUNK, 2 * SGU_CHUNK),
            sgu_b=jnp.repeat(sgu_b[l].T, d // SGU_HEADS, axis=1),
            pool_w=pool_w[l].astype(_BF16), pool_scale=row(pool_scale[l]),
            w_branch=w_branch[l].astype(_BF16), w_out=w_out[l].astype(_BF16),
            w_router_hi=wr_hi, w_router_lo=(w_router[l] - wr_hi.astype(_F32)).astype(_BF16),
        ))
    gf = row(g_final)

    outs = []
    c0 = 0
    for x, c in groups:
        bsz = x.shape[0]
        for l in range(depth):
            mod = mod_all[l, c0:c0 + bsz].reshape(bsz, 6, d)
            x1, h2, aff_tok = _mixer(x, mod, layers[l])
            x = _moe(x1, h2, aff_tok, mod[:, 5:6, :], gf, w_e_gate, w_e_up, w_e_down, l,
                     final_norm=(l == depth - 1))
        outs.append(x)
        c0 += bsz
    return tuple(outs)
```
